```python
import jax, jax.numpy as jnp
from jax import lax
import numpy as np

D_MODEL = 2048
BATCH = 32
SEQ = 256
DEPTH = 1
DEC_BATCH = 2
DEC_SEQ = 4096
PAST_LEN = 512

GRID_W = 64
NA_HEADS = 8
NA_HEAD_DIM = 128
NA_WIN_ROWS = 8
NA_WIN_COLS = 16
MLA_HEADS = 8
MLA_NOPE_DIM = 128
MLA_ROPE_DIM = 64
MLA_V_DIM = 128
Q_LORA = 512
KV_LORA = 512
NA_WIDTH = NA_HEADS * NA_HEAD_DIM
MLA_WIDTH = MLA_HEADS * MLA_V_DIM
MIX_WIDTH = NA_WIDTH + MLA_WIDTH
IN_COLS = 3 * NA_WIDTH + Q_LORA + KV_LORA + MLA_ROPE_DIM
D_FF = -(-8 * D_MODEL // (3 * 256)) * 256
ROPE_THETA = 10000.0
LN_EPS = 1e-5
RMS_EPS = 1e-6
ALPHA = (2.0 * DEPTH) ** 0.25
BETA = (8.0 * DEPTH) ** -0.25
Q_BLOCK = 128
NA_SCALE = NA_HEAD_DIM ** -0.5
MLA_SCALE = (MLA_NOPE_DIM + MLA_ROPE_DIM) ** -0.5

kernel_name = "hybrid_natten_mla_dit_step"


def _layernorm(x, g, b):
    xf = x.astype(jnp.float32)
    mu = xf.mean(-1, keepdims=True)
    var = jnp.square(xf - mu).mean(-1, keepdims=True)
    return ((xf - mu) * lax.rsqrt(var + LN_EPS) * g.astype(jnp.float32) + b.astype(jnp.float32)).astype(x.dtype)


def _rmsnorm(x, g):
    xf = x.astype(jnp.float32)
    return (xf * lax.rsqrt(jnp.mean(xf * xf, -1, keepdims=True) + RMS_EPS) * g.astype(jnp.float32)).astype(x.dtype)


def _modulation(cond, w_mod, b_mod):
    m = jax.nn.silu(cond) @ w_mod + b_mod
    return jnp.split(m[..., None, :], 6, axis=-1)


def _rope_2d(x):
    n = x.shape[1]
    t = jnp.arange(n, dtype=jnp.int32)
    half = MLA_ROPE_DIM // 2
    inv_freq = ROPE_THETA ** (-jnp.arange(0, half, 2, dtype=jnp.float32) / half)

    def rot(xa, pos):
        ang = pos.astype(jnp.float32)[:, None] * inv_freq
        cos = jnp.concatenate([jnp.cos(ang)] * 2, -1)[None, :, None, :]
        sin = jnp.concatenate([jnp.sin(ang)] * 2, -1)[None, :, None, :]
        x1, x2 = jnp.split(xa, 2, -1)
        return xa * cos + jnp.concatenate([-x2, x1], -1) * sin

    xf = x.astype(jnp.float32)
    xr = rot(xf[..., :half], t // GRID_W)
    xc = rot(xf[..., half:], t % GRID_W)
    return jnp.concatenate([xr, xc], -1).astype(x.dtype)


def _project(h, w_in, q_a_g, kv_a_g, w_q_b):
    B, N, _ = h.shape
    z = h @ w_in
    s0 = 3 * NA_WIDTH
    na_qkv, q_lat, kv_lat, k_rope = jnp.split(z, [s0, s0 + Q_LORA, s0 + Q_LORA + KV_LORA], axis=-1)
    na_q, na_k, na_v = [t.reshape(B, N, NA_HEADS, NA_HEAD_DIM) for t in jnp.split(na_qkv, 3, axis=-1)]
    q = (_rmsnorm(q_lat, q_a_g) @ w_q_b).reshape(B, N, MLA_HEADS, MLA_NOPE_DIM + MLA_ROPE_DIM)
    q_nope, q_rope = q[..., :MLA_NOPE_DIM], q[..., MLA_NOPE_DIM:]
    c_kv = _rmsnorm(kv_lat, kv_a_g)
    return na_q, na_k, na_v, q_nope, q_rope, c_kv, k_rope


def _mla_expand(c_kv, w_kv_b):
    B, N, _ = c_kv.shape
    kv = (c_kv @ w_kv_b).reshape(B, N, MLA_HEADS, MLA_NOPE_DIM + MLA_V_DIM)
    return kv[..., :MLA_NOPE_DIM], kv[..., MLA_NOPE_DIM:]


def _attend(q, k, v, scale):
    B, Nq, H, dk = q.shape
    nb = Nq // Q_BLOCK
    qb = q.reshape(B, nb, Q_BLOCK, H, dk).transpose(1, 0, 2, 3, 4)

    def one(qblk):
        s = jnp.einsum('bqhd,bkhd->bhqk', qblk, k).astype(jnp.float32) * scale
        p = jax.nn.softmax(s, axis=-1).astype(v.dtype)
        return jnp.einsum('bhqk,bkhd->bqhd', p, v)

    o = lax.map(one, qb)
    return o.transpose(1, 0, 2, 3, 4).reshape(B, Nq, H, v.shape[-1])


def _neighbourhood_attention(q, k, v, k_ctx, v_ctx, rpb):
    B, N, H, d = q.shape
    rows = N // GRID_W
    wr = min(NA_WIN_ROWS, rows)
    r = jnp.arange(rows)
    rs = jnp.clip(r - wr // 2, 0, rows - wr)
    key_rows = rs[:, None] + jnp.arange(wr)[None, :]
    col = jnp.arange(GRID_W)
    cs = jnp.clip(col - NA_WIN_COLS // 2, 0, GRID_W - NA_WIN_COLS)
    in_win = (col[None, :] >= cs[:, None]) & (col[None, :] < cs[:, None] + NA_WIN_COLS)
    dr = key_rows - r[:, None] + NA_WIN_ROWS - 1
    dc = jnp.clip(col[None, :] - col[:, None] + NA_WIN_COLS - 1, 0, 2 * NA_WIN_COLS - 2)
    bias = rpb[:, dr[:, None, :, None], dc[None, :, None, :]].astype(jnp.float32)

    qg = q.reshape(B, rows, GRID_W, H, d)
    kg = k.reshape(B, rows, GRID_W, H, d)[:, key_rows]
    vg = v.reshape(B, rows, GRID_W, H, d)[:, key_rows]
    s_loc = jnp.einsum('brqhd,brwkhd->bhrqwk', qg, kg).astype(jnp.float32) * NA_SCALE + bias
    s_loc = jnp.where(in_win[:, None, :], s_loc, -jnp.inf)
    s_ctx = jnp.einsum('brqhd,blhd->bhrql', qg, k_ctx).astype(jnp.float32) * NA_SCALE
    nloc = wr * GRID_W
    s = jnp.concatenate([s_loc.reshape(B, H, rows, GRID_W, nloc), s_ctx], axis=-1)
    p = jax.nn.softmax(s, axis=-1).astype(v.dtype)
    p_loc = p[..., :nloc].reshape(B, H, rows, GRID_W, wr, GRID_W)
    p_ctx = p[..., nloc:]
    o = jnp.einsum('bhrqwk,brwkhd->brqhd', p_loc, vg) + jnp.einsum('bhrql,blhd->brqhd', p_ctx, v_ctx)
    return o.reshape(B, N, H, d)


def _merge_and_ffn(x, o_na, o_mla, mod, w_o, ln1_g, ln1_b, w_gu, w_down, ln2_g, ln2_b):
    sh1, sc1, g1, sh2, sc2, g2 = mod
    B, N = x.shape[:2]
    o = jnp.concatenate([o_na.reshape(B, N, NA_WIDTH), o_mla.reshape(B, N, MLA_WIDTH)], axis=-1) @ w_o
    x = _layernorm(ALPHA * x + g1 * o, ln1_g, ln1_b)
    h = x * (1 + sc2) + sh2
    gate, up = jnp.split(h @ w_gu, 2, axis=-1)
    f = (jax.nn.silu(gate) * up) @ w_down
    return _layernorm(ALPHA * x + g2 * f, ln2_g, ln2_b)


def _context_layer(x, mod, w_in, q_a_g, kv_a_g, w_q_b, w_kv_b, w_o, ln1_g, ln1_b, w_gu, w_down, ln2_g, ln2_b):
    sh1, sc1 = mod[0], mod[1]
    h = x * (1 + sc1) + sh1
    na_q, na_k, na_v, q_nope, q_rope, c_kv, k_rope = _project(h, w_in, q_a_g, kv_a_g, w_q_b)
    o_na = _attend(na_q, na_k, na_v, NA_SCALE)
    k_nope, v_m = _mla_expand(c_kv, w_kv_b)
    B, L = x.shape[:2]
    q_m = jnp.concatenate([q_nope, q_rope], axis=-1)
    k_m = jnp.concatenate([k_nope, jnp.broadcast_to(k_rope[:, :, None, :], (B, L, MLA_HEADS, MLA_ROPE_DIM))], axis=-1)
    o_mla = _attend(q_m, k_m, v_m, MLA_SCALE)
    y = _merge_and_ffn(x, o_na, o_mla, mod, w_o, ln1_g, ln1_b, w_gu, w_down, ln2_g, ln2_b)
    return y, na_k, na_v, c_kv, k_rope


def _latent_layer(x, mod, ck, cv, cckv, ckrope, w_in, q_a_g, kv_a_g, w_q_b, w_kv_b, rpb, w_o,
                  ln1_g, ln1_b, w_gu, w_down, ln2_g, ln2_b):
    sh1, sc1 = mod[0], mod[1]
    h = x * (1 + sc1) + sh1
    na_q, na_k, na_v, q_nope, q_rope, c_kv, k_rope = _project(h, w_in, q_a_g, kv_a_g, w_q_b)
    o_na = _neighbourhood_attention(na_q, na_k, na_v, ck, cv, rpb)
    B, N = x.shape[:2]
    L = cckv.shape[1]
    k_nope, v_lat = _mla_expand(c_kv, w_kv_b)
    k_nope_ctx, v_ctx = _mla_expand(cckv, w_kv_b)
    q_m = jnp.concatenate([q_nope, _rope_2d(q_rope)], axis=-1)
    k_rope_lat = jnp.broadcast_to(_rope_2d(k_rope[:, :, None, :]), (B, N, MLA_HEADS, MLA_ROPE_DIM))
    k_rope_ctx = jnp.broadcast_to(ckrope[:, :, None, :], (B, L, MLA_HEADS, MLA_ROPE_DIM))
    k_m = jnp.concatenate([jnp.concatenate([k_nope, k_rope_lat], -1),
                           jnp.concatenate([k_nope_ctx, k_rope_ctx], -1)], axis=1)
    v_m = jnp.concatenate([v_lat, v_ctx], axis=1)
    o_mla = _attend(q_m, k_m, v_m, MLA_SCALE)
    return _merge_and_ffn(x, o_na, o_mla, mod, w_o, ln1_g, ln1_b, w_gu, w_down, ln2_g, ln2_b)


def setup_inputs(seed: int = 0) -> dict:
    key = jax.random.key(seed)
    ks = jax.random.split(key, 32)
    nrm = lambda k, shape, s: jax.random.normal(k, shape, jnp.float32) * s
    return {
        "x_prompt": nrm(ks[0], (BATCH, SEQ, D_MODEL), 1.0),
        "x_sample": nrm(ks[1], (DEC_BATCH, DEC_SEQ, D_MODEL), 1.0),
        "cache_na_k": nrm(ks[2], (DEC_BATCH, DEPTH, PAST_LEN, NA_HEADS, NA_HEAD_DIM), 1.0),
        "cache_na_v": nrm(ks[3], (DEC_BATCH, DEPTH, PAST_LEN, NA_HEADS, NA_HEAD_DIM), 1.0),
        "cache_mla_ckv": nrm(ks[4], (DEC_BATCH, DEPTH, PAST_LEN, KV_LORA), 1.0),
        "cache_mla_krope": nrm(ks[5], (DEC_BATCH, DEPTH, PAST_LEN, MLA_ROPE_DIM), 1.0),
        "c": nrm(ks[6], (DEC_BATCH, D_MODEL), 1.0),
        "c_ctx": nrm(ks[7], (D_MODEL,), 1.0),
        "w_mod": nrm(ks[8], (DEPTH, D_MODEL, 6 * D_MODEL), 0.5 * D_MODEL ** -0.5),
        "b_mod": nrm(ks[9], (DEPTH, 6 * D_MODEL), 0.02),
        "w_in": nrm(ks[10], (DEPTH, D_MODEL, IN_COLS), D_MODEL ** -0.5),
        "q_a_norm": 1.0 + nrm(ks[11], (DEPTH, Q_LORA), 0.02),
        "kv_a_norm": 1.0 + nrm(ks[12], (DEPTH, KV_LORA), 0.02),
        "w_q_b": nrm(ks[13], (DEPTH, Q_LORA, MLA_HEADS * (MLA_NOPE_DIM + MLA_ROPE_DIM)), Q_LORA ** -0.5),
        "w_kv_b": nrm(ks[14], (DEPTH, KV_LORA, MLA_HEADS * (MLA_NOPE_DIM + MLA_V_DIM)), KV_LORA ** -0.5),
        "na_rpb": nrm(ks[15], (DEPTH, NA_HEADS, 2 * NA_WIN_ROWS - 1, 2 * NA_WIN_COLS - 1), 0.1),
        "w_o": nrm(ks[16], (DEPTH, MIX_WIDTH, D_MODEL), BETA * MIX_WIDTH ** -0.5),
        "ln1_g": 1.0 + nrm(ks[17], (DEPTH, D_MODEL), 0.02),
        "ln1_b": nrm(ks[18], (DEPTH, D_MODEL), 0.02),
        "w_gu": nrm(ks[19], (DEPTH, D_MODEL, 2 * D_FF), D_MODEL ** -0.5),
        "w_down": nrm(ks[20], (DEPTH, D_FF, D_MODEL), BETA * D_FF ** -0.5),
        "ln2_g": 1.0 + nrm(ks[21], (DEPTH, D_MODEL), 0.02),
        "ln2_b": nrm(ks[22], (DEPTH, D_MODEL), 0.02),
    }


def reference(x_prompt, x_sample, cache_na_k, cache_na_v, cache_mla_ckv, cache_mla_krope, c, c_ctx,
              w_mod, b_mod, w_in, q_a_norm, kv_a_norm, w_q_b, w_kv_b, na_rpb, w_o,
              ln1_g, ln1_b, w_gu, w_down, ln2_g, ln2_b):
    xp = x_prompt
    nk, nv, nckv, nkr = [], [], [], []
    for l in range(DEPTH):
        mod_ctx = _modulation(c_ctx, w_mod[l], b_mod[l])
        xp, k_l, v_l, ckv_l, kr_l = _context_layer(
            xp, mod_ctx, w_in[l], q_a_norm[l], kv_a_norm[l], w_q_b[l], w_kv_b[l], w_o[l],
            ln1_g[l], ln1_b[l], w_gu[l], w_down[l], ln2_g[l], ln2_b[l])
        nk.append(k_l)
        nv.append(v_l)
        nckv.append(ckv_l)
        nkr.append(kr_l)
    state_na_k = jnp.stack(nk, axis=1)
    state_na_v = jnp.stack(nv, axis=1)
    state_mla_ckv = jnp.stack(nckv, axis=1)
    state_mla_krope = jnp.stack(nkr, axis=1)

    xs = x_sample
    for l in range(DEPTH):
        mod_lat = _modulation(c, w_mod[l], b_mod[l])
        xs = _latent_layer(
            xs, mod_lat, cache_na_k[:, l], cache_na_v[:, l], cache_mla_ckv[:, l], cache_mla_krope[:, l],
            w_in[l], q_a_norm[l], kv_a_norm[l], w_q_b[l], w_kv_b[l], na_rpb[l], w_o[l],
            ln1_g[l], ln1_b[l], w_gu[l], w_down[l], ln2_g[l], ln2_b[l])

    return (xp, xs, state_na_k, state_na_v, state_mla_ckv, state_mla_krope)
```

```python
import functools

import numpy as np
import jax
import jax.numpy as jnp
from jax import lax
from jax.experimental import pallas as pl
from jax.experimental.pallas import tpu as pltpu

D_MODEL = 2048
BATCH = 32
SEQ = 256
DEC_BATCH = 2
DEC_SEQ = 4096
PAST_LEN = 512
GRID_W = 64
GRID_H = DEC_SEQ // GRID_W
NA_HEADS = 8
NA_HEAD_DIM = 128
NA_WIN_ROWS = 8
NA_WIN_COLS = 16
MLA_HEADS = 8
MLA_NOPE_DIM = 128
MLA_ROPE_DIM = 64
MLA_V_DIM = 128
Q_LORA = 512
KV_LORA = 512
NA_WIDTH = NA_HEADS * NA_HEAD_DIM
MLA_WIDTH = MLA_HEADS * MLA_V_DIM
D_FF = -(-8 * D_MODEL // (3 * 256)) * 256
ROPE_THETA = 10000.0
LN_EPS = 1e-5
RMS_EPS = 1e-6
ALPHA = 2.0 ** 0.25
NA_SCALE = NA_HEAD_DIM ** -0.5
MLA_SCALE = (MLA_NOPE_DIM + MLA_ROPE_DIM) ** -0.5

BF = jnp.bfloat16
F32 = jnp.float32
LANES = 128
MLA_QK_PAD = 2 * LANES
NEG = -1e30
VMEM_LIMIT = 60 * 1024 * 1024

N_TOK = BATCH * SEQ
assert N_TOK == DEC_BATCH * DEC_SEQ

PROJ_TM = 512
PROJ_TN = 512
OPROJ_TM = 512
FFN_TM = 1024
FFN_TF = 512
FFN_TN = 512
MLA_TQ = 512
MLA_KB = 512
NA_PAIR = 2 * GRID_W
NA_WIN_TOK = 10 * GRID_W


def _cparams(sem):
    return pltpu.CompilerParams(dimension_semantics=sem, vmem_limit_bytes=VMEM_LIMIT)


def _dot(a, b):
    return jnp.dot(a, b, preferred_element_type=F32)


def _dot_nt(a, b):
    return lax.dot_general(a, b, (((1,), (1,)), ((), ())), preferred_element_type=F32)


def _sigmoid(x):
    return 1.0 / (1.0 + jnp.exp(-x))


def _layernorm(y, g, b):
    mu = jnp.mean(y, axis=-1, keepdims=True)
    yc = y - mu
    var = jnp.mean(yc * yc, axis=-1, keepdims=True)
    return yc * lax.rsqrt(var + LN_EPS) * g + b


def _rmsnorm(x, g):
    return x * lax.rsqrt(jnp.mean(x * x, axis=-1, keepdims=True) + RMS_EPS) * g


def _mod_kernel(c_ref, w_ref, b_ref, o_ref):
    c = c_ref[...]
    s = (c * _sigmoid(c)).astype(BF)
    o_ref[...] = _dot(s, w_ref[...].astype(BF)) + b_ref[...]


def _modulation(cond8, w_mod, b_mod):
    tn = 1024
    n = 6 * D_MODEL
    return pl.pallas_call(
        _mod_kernel,
        out_shape=jax.ShapeDtypeStruct((8, n), F32),
        grid=(n // tn,),
        in_specs=[pl.BlockSpec((8, D_MODEL), lambda j: (0, 0)),
                  pl.BlockSpec((D_MODEL, tn), lambda j: (0, j)),
                  pl.BlockSpec((1, tn), lambda j: (0, j))],
        out_specs=pl.BlockSpec((8, tn), lambda j: (0, j)),
        compiler_params=_cparams(("parallel",)),
        name="modulation",
    )(cond8, w_mod, b_mod)


def _rope128(x, cos, sin):
    lane = lax.broadcasted_iota(jnp.int32, x.shape, 1)
    partner = jnp.where(lane % 32 < 16, pltpu.roll(x, LANES - 16, 1), pltpu.roll(x, 16, 1))
    return x * cos + partner * sin


def _proj_kernel(*refs, rope, state):
    it = iter(refs)
    x_ref, mod_ref, wm_ref, wr_ref, qg_ref, kvg_ref = (next(it) for _ in range(6))
    wqn_ref, wqr_ref, wkn_ref, wv_ref = (next(it) for _ in range(4))
    cos_ref, sin_ref = (next(it), next(it)) if rope else (None, None)
    naq_ref, nak_ref, nav_ref = (next(it) for _ in range(3))
    ckv_ref, kr_ref = (next(it), next(it)) if state else (None, None)
    qm_ref, km_ref, vm_ref, h_scr, kr_scr = (next(it) for _ in range(5))

    j = pl.program_id(1)

    @pl.when(j == 0)
    def _():
        h = x_ref[...] * (1.0 + mod_ref[0, 1:2, :]) + mod_ref[0, 0:1, :]
        hb = h.astype(BF)
        h_scr[...] = hb
        kr = _dot(hb, wr_ref[...])
        if state:
            kr_ref[...] = kr[:, :MLA_ROPE_DIM]
        if rope:
            kr = _rope128(kr, cos_ref[...], sin_ref[...])
        kr_scr[...] = kr

    z = _dot(h_scr[...], wm_ref[...])

    @pl.when(j < 2)
    def _():
        naq_ref[...] = (z * NA_SCALE).astype(naq_ref.dtype)

    @pl.when((j >= 2) & (j < 4))
    def _():
        nak_ref[...] = z.astype(nak_ref.dtype)

    @pl.when((j >= 4) & (j < 6))
    def _():
        nav_ref[...] = z.astype(nav_ref.dtype)

    @pl.when(j == 6)
    def _():
        qn = _rmsnorm(z, qg_ref[...]).astype(BF)
        qnope = _dot(qn, wqn_ref[...]) * MLA_SCALE
        qrope = _dot(qn, wqr_ref[...]) * MLA_SCALE
        for h in range(MLA_HEADS):
            sl = slice(h * LANES, (h + 1) * LANES)
            r = qrope[:, sl]
            if rope:
                r = _rope128(r, cos_ref[...], sin_ref[...])
            qm_ref[:, h * MLA_QK_PAD: h * MLA_QK_PAD + LANES] = qnope[:, sl].astype(BF)
            qm_ref[:, h * MLA_QK_PAD + LANES: (h + 1) * MLA_QK_PAD] = r.astype(BF)

    @pl.when(j == 7)
    def _():
        ckv = _rmsnorm(z, kvg_ref[...])
        if state:
            ckv_ref[...] = ckv
        cb = ckv.astype(BF)
        kn = _dot(cb, wkn_ref[...])
        vm_ref[...] = _dot(cb, wv_ref[...]).astype(BF)
        krb = kr_scr[...].astype(BF)
        for h in range(MLA_HEADS):
            km_ref[:, h * MLA_QK_PAD: h * MLA_QK_PAD + LANES] = kn[:, h * LANES:(h + 1) * LANES].astype(BF)
            km_ref[:, h * MLA_QK_PAD + LANES: (h + 1) * MLA_QK_PAD] = krb


def _projection(x2d, mod3, w, *, rope, state, cos=None, sin=None):
    tm, tn = PROJ_TM, PROJ_TN
    n_i = N_TOK // tm
    tiles_per_batch = DEC_SEQ // tm
    if rope:
        mod_map = lambda i, j: (1 + i // tiles_per_batch, 0, 0)
    else:
        mod_map = lambda i, j: (0, 0, 0)
    const = lambda i, j: (0, 0)
    row = lambda i, j: (i, 0)
    in_specs = [
        pl.BlockSpec((tm, D_MODEL), row),
        pl.BlockSpec((1, 6, D_MODEL), mod_map),
        pl.BlockSpec((D_MODEL, tn), lambda i, j: (0, j)),
        pl.BlockSpec((D_MODEL, LANES), const),
        pl.BlockSpec((1, Q_LORA), const),
        pl.BlockSpec((1, KV_LORA), const),
        pl.BlockSpec((Q_LORA, NA_WIDTH), const),
        pl.BlockSpec((Q_LORA, NA_WIDTH), const),
        pl.BlockSpec((KV_LORA, NA_WIDTH), const),
        pl.BlockSpec((KV_LORA, MLA_WIDTH), const),
    ]
    args = [x2d, mod3, w["w_main"], w["w_rope"], w["q_g"], w["kv_g"], w["w_qn"], w["w_qr"], w["w_kn"], w["w_v"]]
    if rope:
        in_specs += [pl.BlockSpec((tm, LANES), lambda i, j: (i % tiles_per_batch, 0))] * 2
        args += [cos, sin]
    kv_dtype = F32 if state else BF
    out_shape = [jax.ShapeDtypeStruct((N_TOK, NA_WIDTH), BF),
                 jax.ShapeDtypeStruct((N_TOK, NA_WIDTH), kv_dtype),
                 jax.ShapeDtypeStruct((N_TOK, NA_WIDTH), kv_dtype)]
    out_specs = [pl.BlockSpec((tm, tn), lambda i, j: (i, jnp.minimum(j, 1))),
                 pl.BlockSpec((tm, tn), lambda i, j: (i, jnp.clip(j - 2, 0, 1))),
                 pl.BlockSpec((tm, tn), lambda i, j: (i, jnp.clip(j - 4, 0, 1)))]
    if state:
        out_shape += [jax.ShapeDtypeStruct((N_TOK, KV_LORA), F32),
                      jax.ShapeDtypeStruct((N_TOK, MLA_ROPE_DIM), F32)]
        out_specs += [pl.BlockSpec((tm, KV_LORA), row), pl.BlockSpec((tm, MLA_ROPE_DIM), row)]
    out_shape += [jax.ShapeDtypeStruct((N_TOK, MLA_HEADS * MLA_QK_PAD), BF),
                  jax.ShapeDtypeStruct((N_TOK, MLA_HEADS * MLA_QK_PAD), BF),
                  jax.ShapeDtypeStruct((N_TOK, MLA_WIDTH), BF)]
    out_specs += [pl.BlockSpec((tm, MLA_HEADS * MLA_QK_PAD), row),
                  pl.BlockSpec((tm, MLA_HEADS * MLA_QK_PAD), row),
                  pl.BlockSpec((tm, MLA_WIDTH), row)]
    return pl.pallas_call(
        functools.partial(_proj_kernel, rope=rope, state=state),
        out_shape=out_shape,
        grid=(n_i, 4096 // tn),
        in_specs=in_specs,
        out_specs=out_specs,
        scratch_shapes=[pltpu.VMEM((tm, D_MODEL), BF), pltpu.VMEM((tm, LANES), F32)],
        compiler_params=_cparams(("parallel", "arbitrary")),
        name="projection_latent" if rope else "projection_context",
    )(*args)


def _cache_kv_kernel(ckv_ref, kr_ref, wkn_ref, wv_ref, km_ref, vm_ref):
    cb = ckv_ref[...].astype(BF)
    kn = _dot(cb, wkn_ref[...])
    vm_ref[...] = _dot(cb, wv_ref[...]).astype(BF)
    krb = kr_ref[...].astype(BF)
    for h in range(MLA_HEADS):
        km_ref[:, h * MLA_QK_PAD: h * MLA_QK_PAD + LANES] = kn[:, h * LANES:(h + 1) * LANES].astype(BF)
        km_ref[:, h * MLA_QK_PAD + LANES: (h + 1) * MLA_QK_PAD] = krb


def _cache_kv(ckv2d, kr2d, w):
    n = DEC_BATCH * PAST_LEN
    tm = PAST_LEN
    const = lambda i: (0, 0)
    row = lambda i: (i, 0)
    return pl.pallas_call(
        _cache_kv_kernel,
        out_shape=[jax.ShapeDtypeStruct((n, MLA_HEADS * MLA_QK_PAD), BF),
                   jax.ShapeDtypeStruct((n, MLA_WIDTH), BF)],
        grid=(n // tm,),
        in_specs=[pl.BlockSpec((tm, KV_LORA), row), pl.BlockSpec((tm, LANES), row),
                  pl.BlockSpec((KV_LORA, NA_WIDTH), const), pl.BlockSpec((KV_LORA, MLA_WIDTH), const)],
        out_specs=[pl.BlockSpec((tm, MLA_HEADS * MLA_QK_PAD), row), pl.BlockSpec((tm, MLA_WIDTH), row)],
        compiler_params=_cparams(("parallel",)),
        name="cache_kv_expand",
    )(ckv2d, kr2d, w["w_kn"], w["w_v"])


def _softmax_av(s_list, v_list):
    m = functools.reduce(jnp.maximum, [s.max(axis=-1, keepdims=True) for s in s_list])
    p_list = [jnp.exp(s - m) for s in s_list]
    l = functools.reduce(jnp.add, [p.sum(axis=-1, keepdims=True) for p in p_list])
    o = functools.reduce(jnp.add, [_dot(p.astype(BF), v) for p, v in zip(p_list, v_list)])
    return o * (1.0 / l)


def _ctx_attn_kernel(q_ref, k_ref, v_ref, qm_ref, km_ref, vm_ref, ona_ref, omla_ref):
    for h in range(NA_HEADS):
        sl = slice(h * NA_HEAD_DIM, (h + 1) * NA_HEAD_DIM)
        s = _dot_nt(q_ref[:, sl], k_ref[:, sl].astype(BF))
        ona_ref[:, sl] = _softmax_av([s], [v_ref[:, sl].astype(BF)]).astype(BF)
    for h in range(MLA_HEADS):
        sq = slice(h * MLA_QK_PAD, (h + 1) * MLA_QK_PAD)
        sv = slice(h * MLA_V_DIM, (h + 1) * MLA_V_DIM)
        s = _dot_nt(qm_ref[:, sq], km_ref[:, sq])
        omla_ref[:, sv] = _softmax_av([s], [vm_ref[:, sv]]).astype(BF)


def _ctx_attention(naq, nak, nav, qm, km, vm):
    row = lambda b: (b, 0)
    wide = MLA_HEADS * MLA_QK_PAD
    return pl.pallas_call(
        _ctx_attn_kernel,
        out_shape=[jax.ShapeDtypeStruct((N_TOK, NA_WIDTH), BF), jax.ShapeDtypeStruct((N_TOK, MLA_WIDTH), BF)],
        grid=(BATCH,),
        in_specs=[pl.BlockSpec((SEQ, NA_WIDTH), row)] * 3
                 + [pl.BlockSpec((SEQ, wide), row)] * 2 + [pl.BlockSpec((SEQ, MLA_WIDTH), row)],
        out_specs=[pl.BlockSpec((SEQ, NA_WIDTH), row), pl.BlockSpec((SEQ, MLA_WIDTH), row)],
        compiler_params=_cparams(("parallel",)),
        name="context_attention",
    )(naq, nak, nav, qm, km, vm)


def _na_window_start(pair):
    return jnp.clip(2 * pair - NA_WIN_ROWS // 2, 0, GRID_H - NA_WIN_TOK // GRID_W)


def _lat_na_kernel(q_ref, k_ref, v_ref, kc_ref, vc_ref, bias_ref, o_ref):
    start = pl.multiple_of(_na_window_start(pl.program_id(1)) * GRID_W, LANES)
    for h in range(NA_HEADS):
        sl = slice(h * NA_HEAD_DIM, (h + 1) * NA_HEAD_DIM)
        q = q_ref[:, sl]
        s_loc = _dot_nt(q, k_ref[pl.ds(start, NA_WIN_TOK), sl]) + bias_ref[0, h]
        s_ctx = _dot_nt(q, kc_ref[:, sl])
        o = _softmax_av([s_loc, s_ctx], [v_ref[pl.ds(start, NA_WIN_TOK), sl], vc_ref[:, sl]])
        o_ref[:, sl] = o.astype(BF)


def _na_bias_variant(pair):
    last = GRID_H // 2 - 1
    return jnp.where(pair < 2, pair, jnp.where(pair >= last - 1, pair - last + 4, 2))


def _lat_na_attention(naq, nak, nav, kc, vc, bias):
    pairs = GRID_H // 2
    whole = pl.Buffered(1)
    return pl.pallas_call(
        _lat_na_kernel,
        out_shape=jax.ShapeDtypeStruct((N_TOK, NA_WIDTH), BF),
        grid=(DEC_BATCH, pairs),
        in_specs=[pl.BlockSpec((NA_PAIR, NA_WIDTH), lambda b, j: (b * pairs + j, 0)),
                  pl.BlockSpec((DEC_SEQ, NA_WIDTH), lambda b, j: (b, 0), pipeline_mode=whole),
                  pl.BlockSpec((DEC_SEQ, NA_WIDTH), lambda b, j: (b, 0), pipeline_mode=whole),
                  pl.BlockSpec((PAST_LEN, NA_WIDTH), lambda b, j: (b, 0)),
                  pl.BlockSpec((PAST_LEN, NA_WIDTH), lambda b, j: (b, 0)),
                  pl.BlockSpec((1, NA_HEADS, NA_PAIR, NA_WIN_TOK), lambda b, j: (_na_bias_variant(j), 0, 0, 0))],
        out_specs=pl.BlockSpec((NA_PAIR, NA_WIDTH), lambda b, j: (b * pairs + j, 0)),
        compiler_params=_cparams(("parallel", "arbitrary")),
        name="latent_neighbourhood_attention",
    )(naq, nak, nav, kc, vc, bias)


def _lat_mla_kernel(q_ref, k_ref, v_ref, kc_ref, vc_ref, o_ref):
    q = q_ref[...]
    s = _dot_nt(q, kc_ref[...])
    m = s.max(axis=-1, keepdims=True)
    p = jnp.exp(s - m)
    l = p.sum(axis=-1, keepdims=True)
    acc = _dot(p.astype(BF), vc_ref[...])
    for c in range(DEC_SEQ // MLA_KB):
        ks = slice(c * MLA_KB, (c + 1) * MLA_KB)
        s = _dot_nt(q, k_ref[ks, :])
        m_new = jnp.maximum(m, s.max(axis=-1, keepdims=True))
        a = jnp.exp(m - m_new)
        p = jnp.exp(s - m_new)
        l = a * l + p.sum(axis=-1, keepdims=True)
        acc = a * acc + _dot(p.astype(BF), v_ref[ks, :])
        m = m_new
    o_ref[...] = (acc * (1.0 / l)).astype(BF)


def _lat_mla_attention(qm, km, vm, kmc, vmc):
    nq = DEC_SEQ // MLA_TQ
    return pl.pallas_call(
        _lat_mla_kernel,
        out_shape=jax.ShapeDtypeStruct((N_TOK, MLA_WIDTH), BF),
        grid=(DEC_BATCH, MLA_HEADS, nq),
        in_specs=[pl.BlockSpec((MLA_TQ, MLA_QK_PAD), lambda b, h, i: (b * nq + i, h)),
                  pl.BlockSpec((DEC_SEQ, MLA_QK_PAD), lambda b, h, i: (b, h)),
                  pl.BlockSpec((DEC_SEQ, MLA_V_DIM), lambda b, h, i: (b, h)),
                  pl.BlockSpec((PAST_LEN, MLA_QK_PAD), lambda b, h, i: (b, h)),
                  pl.BlockSpec((PAST_LEN, MLA_V_DIM), lambda b, h, i: (b, h))],
        out_specs=pl.BlockSpec((MLA_TQ, MLA_V_DIM), lambda b, h, i: (b * nq + i, h)),
        compiler_params=_cparams(("parallel", "parallel", "arbitrary")),
        name="latent_mla_attention",
    )(qm, km, vm, kmc, vmc)


def _oproj_kernel(ana_ref, amla_ref, x_ref, mod_ref, w1_ref, w2_ref, g_ref, b_ref, o_ref):
    o = _dot(ana_ref[...], w1_ref[...]) + _dot(amla_ref[...], w2_ref[...])
    y = ALPHA * x_ref[...] + mod_ref[0, 2:3, :] * o
    o_ref[...] = _layernorm(y, g_ref[...], b_ref[...])


def _out_projection(ana, amla, x2d, mod3, mod_map1, w_o_na, w_o_mla, g, b):
    tm = OPROJ_TM
    row = lambda i: (i, 0)
    const = lambda i: (0, 0)
    whole = pl.Buffered(1)
    return pl.pallas_call(
        _oproj_kernel,
        out_shape=jax.ShapeDtypeStruct((N_TOK, D_MODEL), F32),
        grid=(N_TOK // tm,),
        in_specs=[pl.BlockSpec((tm, NA_WIDTH), row), pl.BlockSpec((tm, MLA_WIDTH), row),
                  pl.BlockSpec((tm, D_MODEL), row),
                  pl.BlockSpec((1, 6, D_MODEL), lambda i: (mod_map1(i, tm), 0, 0)),
                  pl.BlockSpec((NA_WIDTH, D_MODEL), const, pipeline_mode=whole),
                  pl.BlockSpec((MLA_WIDTH, D_MODEL), const, pipeline_mode=whole),
                  pl.BlockSpec((1, D_MODEL), const), pl.BlockSpec((1, D_MODEL), const)],
        out_specs=pl.BlockSpec((tm, D_MODEL), row),
        compiler_params=_cparams(("parallel",)),
        name="out_projection_ln1",
    )(ana, amla, x2d, mod3, w_o_na, w_o_mla, g, b)


def _ffn_kernel(x_ref, mod_ref, wg_ref, wu_ref, wd_ref, g_ref, b_ref, o_ref, h_scr):
    j = pl.program_id(1)

    @pl.when(j == 0)
    def _():
        h_scr[...] = (x_ref[...] * (1.0 + mod_ref[0, 4:5, :]) + mod_ref[0, 3:4, :]).astype(BF)

    hb = h_scr[...]
    gate = _dot(hb, wg_ref[...])
    up = _dot(hb, wu_ref[...])
    act = (gate * _sigmoid(gate) * up).astype(BF)

    @pl.when(j == 0)
    def _():
        o_ref[...] = jnp.zeros_like(o_ref)

    for n in range(D_MODEL // FFN_TN):
        sl = slice(n * FFN_TN, (n + 1) * FFN_TN)
        o_ref[:, sl] += _dot(act, wd_ref[:, sl])

    @pl.when(j == pl.num_programs(1) - 1)
    def _():
        y = ALPHA * x_ref[...] + mod_ref[0, 5:6, :] * o_ref[...]
        o_ref[...] = _layernorm(y, g_ref[...], b_ref[...])


def _ffn(x1, mod3, mod_map1, w_gu, w_down, g, b):
    tm, tf = FFN_TM, FFN_TF
    nf = D_FF // tf
    const = lambda i, j: (0, 0)
    row = lambda i, j: (i, 0)
    return pl.pallas_call(
        _ffn_kernel,
        out_shape=jax.ShapeDtypeStruct((N_TOK, D_MODEL), F32),
        grid=(N_TOK // tm, nf),
        in_specs=[pl.BlockSpec((tm, D_MODEL), row, pipeline_mode=pl.Buffered(1)),
                  pl.BlockSpec((1, 6, D_MODEL), lambda i, j: (mod_map1(i, tm), 0, 0)),
                  pl.BlockSpec((D_MODEL, tf), lambda i, j: (0, j)),
                  pl.BlockSpec((D_MODEL, tf), lambda i, j: (0, j + nf)),
                  pl.BlockSpec((tf, D_MODEL), lambda i, j: (j, 0)),
                  pl.BlockSpec((1, D_MODEL), const), pl.BlockSpec((1, D_MODEL), const)],
        out_specs=pl.BlockSpec((tm, D_MODEL), row),
        scratch_shapes=[pltpu.VMEM((tm, D_MODEL), BF)],
        compiler_params=_cparams(("parallel", "arbitrary")),
        name="swiglu_ln2",
    )(x1, mod3, w_gu, w_gu, w_down, g, b)


def _prep_weights(w_in, q_a_norm, kv_a_norm, w_q_b, w_kv_b):
    w_in_b = w_in.astype(BF)
    wq = w_q_b.astype(BF).reshape(Q_LORA, MLA_HEADS, MLA_NOPE_DIM + MLA_ROPE_DIM)
    wkv = w_kv_b.astype(BF).reshape(KV_LORA, MLA_HEADS, MLA_NOPE_DIM + MLA_V_DIM)
    pad = LANES - MLA_ROPE_DIM
    return {
        "w_main": w_in_b[:, :4096],
        "w_rope": jnp.pad(w_in_b[:, 4096:], ((0, 0), (0, pad))),
        "q_g": q_a_norm.reshape(1, Q_LORA),
        "kv_g": kv_a_norm.reshape(1, KV_LORA),
        "w_qn": wq[:, :, :MLA_NOPE_DIM].reshape(Q_LORA, NA_WIDTH),
        "w_qr": jnp.pad(wq[:, :, MLA_NOPE_DIM:], ((0, 0), (0, 0), (0, pad))).reshape(Q_LORA, NA_WIDTH),
        "w_kn": wkv[:, :, :MLA_NOPE_DIM].reshape(KV_LORA, NA_WIDTH),
        "w_v": wkv[:, :, MLA_NOPE_DIM:].reshape(KV_LORA, MLA_WIDTH),
    }


def _rope_tables():
    half = MLA_ROPE_DIM // 2
    inv_freq = ROPE_THETA ** (-jnp.arange(0, half, 2, dtype=F32) / half)
    t = jnp.arange(DEC_SEQ, dtype=jnp.int32)

    def tables(pos):
        ang = pos.astype(F32)[:, None] * inv_freq
        s = jnp.sin(ang)
        return jnp.concatenate([jnp.cos(ang)] * 2, -1), jnp.concatenate([-s, s], -1)

    cr, sr = tables(t // GRID_W)
    cc, sc = tables(t % GRID_W)
    pad = LANES - MLA_ROPE_DIM
    cos = jnp.concatenate([cr, cc, jnp.ones((DEC_SEQ, pad), F32)], -1)
    sin = jnp.concatenate([sr, sc, jnp.zeros((DEC_SEQ, pad), F32)], -1)
    return cos, sin


def _na_bias_tables(rpb):
    pairs = GRID_H // 2
    win_rows = NA_WIN_TOK // GRID_W
    reps = (0, 1, 2, pairs - 2, pairs - 1)
    dr = np.zeros((5, 2, win_rows), np.int32)
    ok_row = np.zeros((5, 2, win_rows), bool)
    for v, pair in enumerate(reps):
        start = min(max(2 * pair - NA_WIN_ROWS // 2, 0), GRID_H - win_rows)
        for i in range(2):
            r = 2 * pair + i
            rs = min(max(r - NA_WIN_ROWS // 2, 0), GRID_H - NA_WIN_ROWS)
            for u in range(win_rows):
                kr = start + u
                if rs <= kr < rs + NA_WIN_ROWS:
                    ok_row[v, i, u] = True
                    dr[v, i, u] = kr - r + NA_WIN_ROWS - 1
    col = np.arange(GRID_W)
    cs = np.clip(col - NA_WIN_COLS // 2, 0, GRID_W - NA_WIN_COLS)
    ok_col = (col[None, :] >= cs[:, None]) & (col[None, :] < cs[:, None] + NA_WIN_COLS)
    dc = np.clip(col[None, :] - col[:, None] + NA_WIN_COLS - 1, 0, 2 * NA_WIN_COLS - 2)
    g = rpb.astype(F32)[:, dr[:, :, None, :, None], dc[None, None, :, None, :]]
    ok = ok_row[:, :, None, :, None] & ok_col[None, None, :, None, :]
    g = jnp.where(ok[None], g, NEG)
    g = jnp.transpose(g, (1, 0, 2, 3, 4, 5))
    return g.reshape(5, NA_HEADS, NA_PAIR, NA_WIN_TOK)


def kernel(x_prompt, x_sample, cache_na_k, cache_na_v, cache_mla_ckv, cache_mla_krope, c, c_ctx,
           w_mod, b_mod, w_in, q_a_norm, kv_a_norm, w_q_b, w_kv_b, na_rpb, w_o,
           ln1_g, ln1_b, w_gu, w_down, ln2_g, ln2_b):
    cond8 = jnp.concatenate([c_ctx[None], c, jnp.zeros((8 - 1 - DEC_BATCH, D_MODEL), F32)], 0)
    mod3 = _modulation(cond8, w_mod[0], b_mod[0][None]).reshape(8, 6, D_MODEL)

    w = _prep_weights(w_in[0], q_a_norm[0], kv_a_norm[0], w_q_b[0], w_kv_b[0])
    w_o_b = w_o[0].astype(BF)
    w_gu_b = w_gu[0].astype(BF)
    w_down_b = w_down[0].astype(BF)
    g1, b1 = ln1_g[0][None], ln1_b[0][None]
    g2, b2 = ln2_g[0][None], ln2_b[0][None]
    cos, sin = _rope_tables()

    xp = x_prompt.reshape(N_TOK, D_MODEL)
    xs = x_sample.reshape(N_TOK, D_MODEL)
    ctx_mod = lambda i, tm: 0
    lat_mod = lambda i, tm: 1 + i // (DEC_SEQ // tm)

    naq, st_k, st_v, st_ckv, st_kr, qm, km, vm = _projection(xp, mod3, w, rope=False, state=True)
    a_na, a_mla = _ctx_attention(naq, st_k, st_v, qm, km, vm)
    x1 = _out_projection(a_na, a_mla, xp, mod3, ctx_mod, w_o_b[:NA_WIDTH], w_o_b[NA_WIDTH:], g1, b1)
    yp = _ffn(x1, mod3, ctx_mod, w_gu_b, w_down_b, g2, b2)

    naq, nak, nav, qm, km, vm = _projection(xs, mod3, w, rope=True, state=False, cos=cos, sin=sin)
    kr_cache = jnp.pad(cache_mla_krope[:, 0].reshape(DEC_BATCH * PAST_LEN, MLA_ROPE_DIM),
                       ((0, 0), (0, LANES - MLA_ROPE_DIM)))
    kmc, vmc = _cache_kv(cache_mla_ckv[:, 0].reshape(DEC_BATCH * PAST_LEN, KV_LORA), kr_cache, w)
    kc = cache_na_k[:, 0].reshape(DEC_BATCH * PAST_LEN, NA_WIDTH).astype(BF)
    vc = cache_na_v[:, 0].reshape(DEC_BATCH * PAST_LEN, NA_WIDTH).astype(BF)
    a_na = _lat_na_attention(naq, nak, nav, kc, vc, _na_bias_tables(na_rpb[0]))
    a_mla = _lat_mla_attention(qm, km, vm, kmc, vmc)
    x1 = _out_projection(a_na, a_mla, xs, mod3, lat_mod, w_o_b[:NA_WIDTH], w_o_b[NA_WIDTH:], g1, b1)
    ys = _ffn(x1, mod3, lat_mod, w_gu_b, w_down_b, g2, b2)

    return (yp.reshape(BATCH, SEQ, D_MODEL),
            ys.reshape(DEC_BATCH, DEC_SEQ, D_MODEL),
            st_k.reshape(BATCH, 1, SEQ, NA_HEADS, NA_HEAD_DIM),
            st_v.reshape(BATCH, 1, SEQ, NA_HEADS, NA_HEAD_DIM),
            st_ckv.reshape(BATCH, 1, SEQ, KV_LORA),
            st_kr.reshape(BATCH, 1, SEQ, MLA_ROPE_DIM))
```

```python
import functools

import numpy as np
import jax
import jax.numpy as jnp
from jax import lax
from jax.experimental import pallas as pl
from jax.experimental.pallas import tpu as pltpu

D_MODEL = 2048
BATCH = 32
SEQ = 256
DEC_BATCH = 2
DEC_SEQ = 4096
PAST_LEN = 512
GRID_W = 64
GRID_H = DEC_SEQ // GRID_W
NA_HEADS = 8
NA_HEAD_DIM = 128
NA_WIN_ROWS = 8
NA_WIN_COLS = 16
MLA_HEADS = 8
MLA_NOPE_DIM = 128
MLA_ROPE_DIM = 64
MLA_V_DIM = 128
Q_LORA = 512
KV_LORA = 512
NA_WIDTH = NA_HEADS * NA_HEAD_DIM
MLA_WIDTH = MLA_HEADS * MLA_V_DIM
D_FF = -(-8 * D_MODEL // (3 * 256)) * 256
ROPE_THETA = 10000.0
LN_EPS = 1e-5
RMS_EPS = 1e-6
ALPHA = 2.0 ** 0.25
NA_SCALE = NA_HEAD_DIM ** -0.5
MLA_SCALE = (MLA_NOPE_DIM + MLA_ROPE_DIM) ** -0.5

BF = jnp.bfloat16
F32 = jnp.float32
LANES = 128
MLA_QK_PAD = 2 * LANES
NEG = -1e30
VMEM_LIMIT = 60 * 1024 * 1024

N_TOK = BATCH * SEQ
assert N_TOK == DEC_BATCH * DEC_SEQ

PROJ_TM = 512
PROJ_TN = 512
OPROJ_TM = 512
FFN_TM = 1024
FFN_TF = 512
FFN_TN = 512
MLA_TQ = 512
MLA_KB = 512
NA_PAIR = 2 * GRID_W
NA_WIN_TOK = 10 * GRID_W


def _cparams(sem):
    return pltpu.CompilerParams(dimension_semantics=sem, vmem_limit_bytes=VMEM_LIMIT)


def _dot(a, b):
    return jnp.dot(a, b, preferred_element_type=F32)


def _dot_nt(a, b):
    return lax.dot_general(a, b, (((1,), (1,)), ((), ())), preferred_element_type=F32)


def _sigmoid(x):
    return 1.0 / (1.0 + jnp.exp(-x))


def _layernorm(y, g, b):
    mu = jnp.mean(y, axis=-1, keepdims=True)
    yc = y - mu
    var = jnp.mean(yc * yc, axis=-1, keepdims=True)
    return yc * lax.rsqrt(var + LN_EPS) * g + b


def _rmsnorm(x, g):
    return x * lax.rsqrt(jnp.mean(x * x, axis=-1, keepdims=True) + RMS_EPS) * g


def _mod_kernel(c_ref, w_ref, b_ref, o_ref):
    c = c_ref[...]
    s = (c * _sigmoid(c)).astype(BF)
    o_ref[...] = _dot(s, w_ref[...].astype(BF)) + b_ref[...]


def _modulation(cond8, w_mod, b_mod):
    tn = 1024
    n = 6 * D_MODEL
    return pl.pallas_call(
        _mod_kernel,
        out_shape=jax.ShapeDtypeStruct((8, n), F32),
        grid=(n // tn,),
        in_specs=[pl.BlockSpec((8, D_MODEL), lambda j: (0, 0)),
                  pl.BlockSpec((D_MODEL, tn), lambda j: (0, j)),
                  pl.BlockSpec((1, tn), lambda j: (0, j))],
        out_specs=pl.BlockSpec((8, tn), lambda j: (0, j)),
        compiler_params=_cparams(("parallel",)),
        name="modulation",
    )(cond8, w_mod, b_mod)


def _rope128(x, cos, sin):
    lane = lax.broadcasted_iota(jnp.int32, x.shape, 1)
    partner = jnp.where(lane % 32 < 16, pltpu.roll(x, LANES - 16, 1), pltpu.roll(x, 16, 1))
    return x * cos + partner * sin


def _proj_kernel(*refs, rope, state):
    it = iter(refs)
    x_ref, mod_ref, wm_ref, wr_ref, qg_ref, kvg_ref = (next(it) for _ in range(6))
    wqn_ref, wqr_ref, wkn_ref, wv_ref = (next(it) for _ in range(4))
    cos_ref, sin_ref = (next(it), next(it)) if rope else (None, None)
    naq_ref, nak_ref, nav_ref = (next(it) for _ in range(3))
    ckv_ref, kr_ref = (next(it), next(it)) if state else (None, None)
    qm_ref, km_ref, vm_ref, h_scr, kr_scr = (next(it) for _ in range(5))

    j = pl.program_id(1)

    @pl.when(j == 0)
    def _():
        h = x_ref[...] * (1.0 + mod_ref[0, 1:2, :]) + mod_ref[0, 0:1, :]
        hb = h.astype(BF)
        h_scr[...] = hb
        kr = _dot(hb, wr_ref[...])
        if state:
            kr_ref[...] = kr[:, :MLA_ROPE_DIM]
        if rope:
            kr = _rope128(kr, cos_ref[...], sin_ref[...])
        kr_scr[...] = kr

    z = _dot(h_scr[...], wm_ref[...])

    @pl.when(j < 2)
    def _():
        naq_ref[...] = (z * NA_SCALE).astype(naq_ref.dtype)

    @pl.when((j >= 2) & (j < 4))
    def _():
        nak_ref[...] = z.astype(nak_ref.dtype)

    @pl.when((j >= 4) & (j < 6))
    def _():
        nav_ref[...] = z.astype(nav_ref.dtype)

    @pl.when(j == 6)
    def _():
        qn = _rmsnorm(z, qg_ref[...]).astype(BF)
        qnope = _dot(qn, wqn_ref[...]) * MLA_SCALE
        qrope = _dot(qn, wqr_ref[...]) * MLA_SCALE
        for h in range(MLA_HEADS):
            sl = slice(h * LANES, (h + 1) * LANES)
            r = qrope[:, sl]
            if rope:
                r = _rope128(r, cos_ref[...], sin_ref[...])
            qm_ref[:, h * MLA_QK_PAD: h * MLA_QK_PAD + LANES] = qnope[:, sl].astype(BF)
            qm_ref[:, h * MLA_QK_PAD + LANES: (h + 1) * MLA_QK_PAD] = r.astype(BF)

    @pl.when(j == 7)
    def _():
        ckv = _rmsnorm(z, kvg_ref[...])
        if state:
            ckv_ref[...] = ckv
        cb = ckv.astype(BF)
        kn = _dot(cb, wkn_ref[...])
        vm_ref[...] = _dot(cb, wv_ref[...]).astype(BF)
        krb = kr_scr[...].astype(BF)
        for h in range(MLA_HEADS):
            km_ref[:, h * MLA_QK_PAD: h * MLA_QK_PAD + LANES] = kn[:, h * LANES:(h + 1) * LANES].astype(BF)
            km_ref[:, h * MLA_QK_PAD + LANES: (h + 1) * MLA_QK_PAD] = krb


def _projection(x2d, mod3, w, *, rope, state, cos=None, sin=None):
    tm, tn = PROJ_TM, PROJ_TN
    n_i = N_TOK // tm
    tiles_per_batch = DEC_SEQ // tm
    if rope:
        mod_map = lambda i, j: (1 + i // tiles_per_batch, 0, 0)
    else:
        mod_map = lambda i, j: (0, 0, 0)
    const = lambda i, j: (0, 0)
    row = lambda i, j: (i, 0)
    in_specs = [
        pl.BlockSpec((tm, D_MODEL), row),
        pl.BlockSpec((1, 6, D_MODEL), mod_map),
        pl.BlockSpec((D_MODEL, tn), lambda i, j: (0, j)),
        pl.BlockSpec((D_MODEL, LANES), const),
        pl.BlockSpec((1, Q_LORA), const),
        pl.BlockSpec((1, KV_LORA), const),
        pl.BlockSpec((Q_LORA, NA_WIDTH), const),
        pl.BlockSpec((Q_LORA, NA_WIDTH), const),
        pl.BlockSpec((KV_LORA, NA_WIDTH), const),
        pl.BlockSpec((KV_LORA, MLA_WIDTH), const),
    ]
    args = [x2d, mod3, w["w_main"], w["w_rope"], w["q_g"], w["kv_g"], w["w_qn"], w["w_qr"], w["w_kn"], w["w_v"]]
    if rope:
        in_specs += [pl.BlockSpec((tm, LANES), lambda i, j: (i % tiles_per_batch, 0))] * 2
        args += [cos, sin]
    kv_dtype = F32 if state else BF
    out_shape = [jax.ShapeDtypeStruct((N_TOK, NA_WIDTH), BF),
                 jax.ShapeDtypeStruct((N_TOK, NA_WIDTH), kv_dtype),
                 jax.ShapeDtypeStruct((N_TOK, NA_WIDTH), kv_dtype)]
    out_specs = [pl.BlockSpec((tm, tn), lambda i, j: (i, jnp.minimum(j, 1))),
                 pl.BlockSpec((tm, tn), lambda i, j: (i, jnp.clip(j - 2, 0, 1))),
                 pl.BlockSpec((tm, tn), lambda i, j: (i, jnp.clip(j - 4, 0, 1)))]
    if state:
        out_shape += [jax.ShapeDtypeStruct((N_TOK, KV_LORA), F32),
                      jax.ShapeDtypeStruct((N_TOK, MLA_ROPE_DIM), F32)]
        out_specs += [pl.BlockSpec((tm, KV_LORA), row), pl.BlockSpec((tm, MLA_ROPE_DIM), row)]
    out_shape += [jax.ShapeDtypeStruct((N_TOK, MLA_HEADS * MLA_QK_PAD), BF),
                  jax.ShapeDtypeStruct((N_TOK, MLA_HEADS * MLA_QK_PAD), BF),
                  jax.ShapeDtypeStruct((N_TOK, MLA_WIDTH), BF)]
    out_specs += [pl.BlockSpec((tm, MLA_HEADS * MLA_QK_PAD), row),
                  pl.BlockSpec((tm, MLA_HEADS * MLA_QK_PAD), row),
                  pl.BlockSpec((tm, MLA_WIDTH), row)]
    return pl.pallas_call(
        functools.partial(_proj_kernel, rope=rope, state=state),
        out_shape=out_shape,
        grid=(n_i, 4096 // tn),
        in_specs=in_specs,
        out_specs=out_specs,
        scratch_shapes=[pltpu.VMEM((tm, D_MODEL), BF), pltpu.VMEM((tm, LANES), F32)],
        compiler_params=_cparams(("parallel", "arbitrary")),
        name="projection_latent" if rope else "projection_context",
    )(*args)


def _cache_kv_kernel(ckv_ref, kr_ref, wkn_ref, wv_ref, km_ref, vm_ref):
    cb = ckv_ref[...].astype(BF)
    kn = _dot(cb, wkn_ref[...])
    vm_ref[...] = _dot(cb, wv_ref[...]).astype(BF)
    krb = kr_ref[...].astype(BF)
    for h in range(MLA_HEADS):
        km_ref[:, h * MLA_QK_PAD: h * MLA_QK_PAD + LANES] = kn[:, h * LANES:(h + 1) * LANES].astype(BF)
        km_ref[:, h * MLA_QK_PAD + LANES: (h + 1) * MLA_QK_PAD] = krb


def _cache_kv(ckv2d, kr2d, w):
    n = DEC_BATCH * PAST_LEN
    tm = PAST_LEN
    const = lambda i: (0, 0)
    row = lambda i: (i, 0)
    return pl.pallas_call(
        _cache_kv_kernel,
        out_shape=[jax.ShapeDtypeStruct((n, MLA_HEADS * MLA_QK_PAD), BF),
                   jax.ShapeDtypeStruct((n, MLA_WIDTH), BF)],
        grid=(n // tm,),
        in_specs=[pl.BlockSpec((tm, KV_LORA), row), pl.BlockSpec((tm, LANES), row),
                  pl.BlockSpec((KV_LORA, NA_WIDTH), const), pl.BlockSpec((KV_LORA, MLA_WIDTH), const)],
        out_specs=[pl.BlockSpec((tm, MLA_HEADS * MLA_QK_PAD), row), pl.BlockSpec((tm, MLA_WIDTH), row)],
        compiler_params=_cparams(("parallel",)),
        name="cache_kv_expand",
    )(ckv2d, kr2d, w["w_kn"], w["w_v"])


def _softmax_av(s_list, v_list):
    m = functools.reduce(jnp.maximum, [s.max(axis=-1, keepdims=True) for s in s_list])
    p_list = [jnp.exp(s - m) for s in s_list]
    l = functools.reduce(jnp.add, [p.sum(axis=-1, keepdims=True) for p in p_list])
    o = functools.reduce(jnp.add, [_dot(p.astype(BF), v) for p, v in zip(p_list, v_list)])
    return o * (1.0 / l)


def _ctx_attn_kernel(q_ref, k_ref, v_ref, qm_ref, km_ref, vm_ref, ona_ref, omla_ref):
    for h in range(NA_HEADS):
        sl = slice(h * NA_HEAD_DIM, (h + 1) * NA_HEAD_DIM)
        s = _dot_nt(q_ref[:, sl], k_ref[:, sl].astype(BF))
        ona_ref[:, sl] = _softmax_av([s], [v_ref[:, sl].astype(BF)]).astype(BF)
    for h in range(MLA_HEADS):
        sq = slice(h * MLA_QK_PAD, (h + 1) * MLA_QK_PAD)
        sv = slice(h * MLA_V_DIM, (h + 1) * MLA_V_DIM)
        s = _dot_nt(qm_ref[:, sq], km_ref[:, sq])
        omla_ref[:, sv] = _softmax_av([s], [vm_ref[:, sv]]).astype(BF)


def _ctx_attention(naq, nak, nav, qm, km, vm):
    row = lambda b: (b, 0)
    wide = MLA_HEADS * MLA_QK_PAD
    return pl.pallas_call(
        _ctx_attn_kernel,
        out_shape=[jax.ShapeDtypeStruct((N_TOK, NA_WIDTH), BF), jax.ShapeDtypeStruct((N_TOK, MLA_WIDTH), BF)],
        grid=(BATCH,),
        in_specs=[pl.BlockSpec((SEQ, NA_WIDTH), row)] * 3
                 + [pl.BlockSpec((SEQ, wide), row)] * 2 + [pl.BlockSpec((SEQ, MLA_WIDTH), row)],
        out_specs=[pl.BlockSpec((SEQ, NA_WIDTH), row), pl.BlockSpec((SEQ, MLA_WIDTH), row)],
        compiler_params=_cparams(("parallel",)),
        name="context_attention",
    )(naq, nak, nav, qm, km, vm)


def _na_window_start(pair):
    return jnp.clip(2 * pair - NA_WIN_ROWS // 2, 0, GRID_H - NA_WIN_TOK // GRID_W)


def _lat_na_kernel(q_ref, k_ref, v_ref, kc_ref, vc_ref, bias_ref, o_ref):
    start = pl.multiple_of(_na_window_start(pl.program_id(1)) * GRID_W, LANES)
    for h in range(NA_HEADS):
        sl = slice(h * NA_HEAD_DIM, (h + 1) * NA_HEAD_DIM)
        q = q_ref[:, sl]
        s_loc = _dot_nt(q, k_ref[pl.ds(start, NA_WIN_TOK), sl]) + bias_ref[0, h]
        s_ctx = _dot_nt(q, kc_ref[:, sl])
        o = _softmax_av([s_loc, s_ctx], [v_ref[pl.ds(start, NA_WIN_TOK), sl], vc_ref[:, sl]])
        o_ref[:, sl] = o.astype(BF)


def _na_bias_variant(pair):
    last = GRID_H // 2 - 1
    return jnp.where(pair < 2, pair, jnp.where(pair >= last - 1, pair - last + 4, 2))


def _lat_na_attention(naq, nak, nav, kc, vc, bias):
    pairs = GRID_H // 2
    whole = pl.Buffered(1)
    return pl.pallas_call(
        _lat_na_kernel,
        out_shape=jax.ShapeDtypeStruct((N_TOK, NA_WIDTH), BF),
        grid=(DEC_BATCH, pairs),
        in_specs=[pl.BlockSpec((NA_PAIR, NA_WIDTH), lambda b, j: (b * pairs + j, 0)),
                  pl.BlockSpec((DEC_SEQ, NA_WIDTH), lambda b, j: (b, 0), pipeline_mode=whole),
                  pl.BlockSpec((DEC_SEQ, NA_WIDTH), lambda b, j: (b, 0), pipeline_mode=whole),
                  pl.BlockSpec((PAST_LEN, NA_WIDTH), lambda b, j: (b, 0)),
                  pl.BlockSpec((PAST_LEN, NA_WIDTH), lambda b, j: (b, 0)),
                  pl.BlockSpec((1, NA_HEADS, NA_PAIR, NA_WIN_TOK), lambda b, j: (_na_bias_variant(j), 0, 0, 0))],
        out_specs=pl.BlockSpec((NA_PAIR, NA_WIDTH), lambda b, j: (b * pairs + j, 0)),
        compiler_params=_cparams(("parallel", "arbitrary")),
        name="latent_neighbourhood_attention",
    )(naq, nak, nav, kc, vc, bias)


def _lat_mla_kernel(q_ref, k_ref, v_ref, kc_ref, vc_ref, o_ref):
    q = q_ref[...]
    s = _dot_nt(q, kc_ref[...])
    m = s.max(axis=-1, keepdims=True)
    p = jnp.exp(s - m)
    l = p.sum(axis=-1, keepdims=True)
    acc = _dot(p.astype(BF), vc_ref[...])
    for c in range(DEC_SEQ // MLA_KB):
        ks = slice(c * MLA_KB, (c + 1) * MLA_KB)
        s = _dot_nt(q, k_ref[ks, :])
        m_new = jnp.maximum(m, s.max(axis=-1, keepdims=True))
        a = jnp.exp(m - m_new)
        p = jnp.exp(s - m_new)
        l = a * l + p.sum(axis=-1, keepdims=True)
        acc = a * acc + _dot(p.astype(BF), v_ref[ks, :])
        m = m_new
    o_ref[...] = (acc * (1.0 / l)).astype(BF)


def _lat_mla_attention(qm, km, vm, kmc, vmc):
    nq = DEC_SEQ // MLA_TQ
    return pl.pallas_call(
        _lat_mla_kernel,
        out_shape=jax.ShapeDtypeStruct((N_TOK, MLA_WIDTH), BF),
        grid=(DEC_BATCH, MLA_HEADS, nq),
        in_specs=[pl.BlockSpec((MLA_TQ, MLA_QK_PAD), lambda b, h, i: (b * nq + i, h)),
                  pl.BlockSpec((DEC_SEQ, MLA_QK_PAD), lambda b, h, i: (b, h)),
                  pl.BlockSpec((DEC_SEQ, MLA_V_DIM), lambda b, h, i: (b, h)),
                  pl.BlockSpec((PAST_LEN, MLA_QK_PAD), lambda b, h, i: (b, h)),
                  pl.BlockSpec((PAST_LEN, MLA_V_DIM), lambda b, h, i: (b, h))],
        out_specs=pl.BlockSpec((MLA_TQ, MLA_V_DIM), lambda b, h, i: (b * nq + i, h)),
        compiler_params=_cparams(("parallel", "parallel", "arbitrary")),
        name="latent_mla_attention",
    )(qm, km, vm, kmc, vmc)


def _oproj_kernel(ana_ref, amla_ref, x_ref, mod_ref, w1_ref, w2_ref, g_ref, b_ref, o_ref):
    o = _dot(ana_ref[...], w1_ref[...]) + _dot(amla_ref[...], w2_ref[...])
    y = ALPHA * x_ref[...] + mod_ref[0, 2:3, :] * o
    o_ref[...] = _layernorm(y, g_ref[...], b_ref[...])


def _out_projection(ana, amla, x2d, mod3, mod_map1, w_o_na, w_o_mla, g, b):
    tm = OPROJ_TM
    row = lambda i: (i, 0)
    const = lambda i: (0, 0)
    whole = pl.Buffered(1)
    return pl.pallas_call(
        _oproj_kernel,
        out_shape=jax.ShapeDtypeStruct((N_TOK, D_MODEL), F32),
        grid=(N_TOK // tm,),
        in_specs=[pl.BlockSpec((tm, NA_WIDTH), row), pl.BlockSpec((tm, MLA_WIDTH), row),
                  pl.BlockSpec((tm, D_MODEL), row),
                  pl.BlockSpec((1, 6, D_MODEL), lambda i: (mod_map1(i, tm), 0, 0)),
                  pl.BlockSpec((NA_WIDTH, D_MODEL), const, pipeline_mode=whole),
                  pl.BlockSpec((MLA_WIDTH, D_MODEL), const, pipeline_mode=whole),
                  pl.BlockSpec((1, D_MODEL), const), pl.BlockSpec((1, D_MODEL), const)],
        out_specs=pl.BlockSpec((tm, D_MODEL), row),
        compiler_params=_cparams(("parallel",)),
        name="out_projection_ln1",
    )(ana, amla, x2d, mod3, w_o_na, w_o_mla, g, b)


def _ffn_kernel(x_ref, mod_ref, wg_ref, wu_ref, wd_ref, g_ref, b_ref, o_ref, h_scr):
    j = pl.program_id(1)

    @pl.when(j == 0)
    def _():
        h_scr[...] = (x_ref[...] * (1.0 + mod_ref[0, 4:5, :]) + mod_ref[0, 3:4, :]).astype(BF)

    hb = h_scr[...]
    gate = _dot(hb, wg_ref[...])
    up = _dot(hb, wu_ref[...])
    act = (gate * _sigmoid(gate) * up).astype(BF)

    @pl.when(j == 0)
    def _():
        o_ref[...] = jnp.zeros_like(o_ref)

    for n in range(D_MODEL // FFN_TN):
        sl = slice(n * FFN_TN, (n + 1) * FFN_TN)
        o_ref[:, sl] += _dot(act, wd_ref[:, sl])

    @pl.when(j == pl.num_programs(1) - 1)
    def _():
        y = ALPHA * x_ref[...] + mod_ref[0, 5:6, :] * o_ref[...]
        o_ref[...] = _layernorm(y, g_ref[...], b_ref[...])


def _ffn(x1, mod3, mod_map1, w_gu, w_down, g, b):
    tm, tf = FFN_TM, FFN_TF
    nf = D_FF // tf
    const = lambda i, j: (0, 0)
    row = lambda i, j: (i, 0)
    return pl.pallas_call(
        _ffn_kernel,
        out_shape=jax.ShapeDtypeStruct((N_TOK, D_MODEL), F32),
        grid=(N_TOK // tm, nf),
        in_specs=[pl.BlockSpec((tm, D_MODEL), row, pipeline_mode=pl.Buffered(1)),
                  pl.BlockSpec((1, 6, D_MODEL), lambda i, j: (mod_map1(i, tm), 0, 0)),
                  pl.BlockSpec((D_MODEL, tf), lambda i, j: (0, j)),
                  pl.BlockSpec((D_MODEL, tf), lambda i, j: (0, j + nf)),
                  pl.BlockSpec((tf, D_MODEL), lambda i, j: (j, 0)),
                  pl.BlockSpec((1, D_MODEL), const), pl.BlockSpec((1, D_MODEL), const)],
        out_specs=pl.BlockSpec((tm, D_MODEL), row),
        scratch_shapes=[pltpu.VMEM((tm, D_MODEL), BF)],
        compiler_params=_cparams(("parallel", "arbitrary")),
        name="swiglu_ln2",
    )(x1, mod3, w_gu, w_gu, w_down, g, b)


def _prep_weights(w_in, q_a_norm, kv_a_norm, w_q_b, w_kv_b):
    w_in_b = w_in.astype(BF)
    wq = w_q_b.astype(BF).reshape(Q_LORA, MLA_HEADS, MLA_NOPE_DIM + MLA_ROPE_DIM)
    wkv = w_kv_b.astype(BF).reshape(KV_LORA, MLA_HEADS, MLA_NOPE_DIM + MLA_V_DIM)
    pad = LANES - MLA_ROPE_DIM
    return {
        "w_main": w_in_b[:, :4096],
        "w_rope": jnp.pad(w_in_b[:, 4096:], ((0, 0), (0, pad))),
        "q_g": q_a_norm.reshape(1, Q_LORA),
        "kv_g": kv_a_norm.reshape(1, KV_LORA),
        "w_qn": wq[:, :, :MLA_NOPE_DIM].reshape(Q_LORA, NA_WIDTH),
        "w_qr": jnp.pad(wq[:, :, MLA_NOPE_DIM:], ((0, 0), (0, 0), (0, pad))).reshape(Q_LORA, NA_WIDTH),
        "w_kn": wkv[:, :, :MLA_NOPE_DIM].reshape(KV_LORA, NA_WIDTH),
        "w_v": wkv[:, :, MLA_NOPE_DIM:].reshape(KV_LORA, MLA_WIDTH),
    }


def _rope_tables():
    half = MLA_ROPE_DIM // 2
    inv_freq = ROPE_THETA ** (-jnp.arange(0, half, 2, dtype=F32) / half)
    t = jnp.arange(DEC_SEQ, dtype=jnp.int32)

    def tables(pos):
        ang = pos.astype(F32)[:, None] * inv_freq
        s = jnp.sin(ang)
        return jnp.concatenate([jnp.cos(ang)] * 2, -1), jnp.concatenate([-s, s], -1)

    cr, sr = tables(t // GRID_W)
    cc, sc = tables(t % GRID_W)
    pad = LANES - MLA_ROPE_DIM
    cos = jnp.concatenate([cr, cc, jnp.ones((DEC_SEQ, pad), F32)], -1)
    sin = jnp.concatenate([sr, sc, jnp.zeros((DEC_SEQ, pad), F32)], -1)
    return cos, sin


def _na_bias_plan():
    pairs = GRID_H // 2
    win_rows = NA_WIN_TOK // GRID_W
    plan = []
    for pair in (0, 1, 2, pairs - 2, pairs - 1):
        start = min(max(2 * pair - NA_WIN_ROWS // 2, 0), GRID_H - win_rows)
        per_row = []
        for i in range(2):
            r = 2 * pair + i
            rs = min(max(r - NA_WIN_ROWS // 2, 0), GRID_H - NA_WIN_ROWS)
            per_row.append([(start + u - r + NA_WIN_ROWS - 1, rs <= start + u < rs + NA_WIN_ROWS)
                            for u in range(win_rows)])
        plan.append(per_row)
    return plan


def _na_bias_kernel(ra_ref, rb_ref, o_ref):
    shape = (GRID_W, LANES)
    lane = lax.broadcasted_iota(jnp.int32, shape, 1)
    qc = lax.broadcasted_iota(jnp.int32, shape, 0)
    kc = lane % GRID_W
    cs = jnp.clip(qc - NA_WIN_COLS // 2, 0, GRID_W - NA_WIN_COLS)
    ok_col = (kc >= cs) & (kc < cs + NA_WIN_COLS)
    left = lane < GRID_W
    neg = jnp.full(shape, NEG, F32)
    for v, per_row in enumerate(_na_bias_plan()):
        for i, rows in enumerate(per_row):
            for t in range(len(rows) // 2):
                (dra, oka), (drb, okb) = rows[2 * t], rows[2 * t + 1]
                if oka and okb:
                    ok = ok_col
                elif oka:
                    ok = ok_col & left
                elif okb:
                    ok = ok_col & jnp.logical_not(left)
                for h in range(NA_HEADS):
                    if oka or okb:
                        src = jnp.zeros((1, LANES), F32)
                        if oka:
                            src = src + ra_ref[h, dra:dra + 1, :]
                        if okb:
                            src = src + rb_ref[h, drb:drb + 1, :]
                        val = pltpu.roll(jnp.broadcast_to(src, shape), LANES - (NA_WIN_COLS - 1), 1,
                                         stride=1, stride_axis=0)
                        tile = jnp.where(ok, val, neg)
                    else:
                        tile = neg
                    o_ref[v, h, i * GRID_W:(i + 1) * GRID_W, t * LANES:(t + 1) * LANES] = tile


def _na_bias_tables(rpb):
    n_dr, n_dc = 2 * NA_WIN_ROWS - 1, 2 * NA_WIN_COLS - 1
    ra = jnp.pad(rpb, ((0, 0), (0, 16 - n_dr), (0, LANES - n_dc)))
    rb = jnp.pad(rpb, ((0, 0), (0, 16 - n_dr), (GRID_W, LANES - GRID_W - n_dc)))
    return pl.pallas_call(
        _na_bias_kernel,
        out_shape=jax.ShapeDtypeStruct((5, NA_HEADS, NA_PAIR, NA_WIN_TOK), F32),
        compiler_params=pltpu.CompilerParams(vmem_limit_bytes=VMEM_LIMIT),
        name="na_bias_tables",
    )(ra, rb)


def kernel(x_prompt, x_sample, cache_na_k, cache_na_v, cache_mla_ckv, cache_mla_krope, c, c_ctx,
           w_mod, b_mod, w_in, q_a_norm, kv_a_norm, w_q_b, w_kv_b, na_rpb, w_o,
           ln1_g, ln1_b, w_gu, w_down, ln2_g, ln2_b):
    cond8 = jnp.concatenate([c_ctx[None], c, jnp.zeros((8 - 1 - DEC_BATCH, D_MODEL), F32)], 0)
    mod3 = _modulation(cond8, w_mod[0], b_mod[0][None]).reshape(8, 6, D_MODEL)

    w = _prep_weights(w_in[0], q_a_norm[0], kv_a_norm[0], w_q_b[0], w_kv_b[0])
    w_o_b = w_o[0].astype(BF)
    w_gu_b = w_gu[0].astype(BF)
    w_down_b = w_down[0].astype(BF)
    g1, b1 = ln1_g[0][None], ln1_b[0][None]
    g2, b2 = ln2_g[0][None], ln2_b[0][None]
    cos, sin = _rope_tables()

    xp = x_prompt.reshape(N_TOK, D_MODEL)
    xs = x_sample.reshape(N_TOK, D_MODEL)
    ctx_mod = lambda i, tm: 0
    lat_mod = lambda i, tm: 1 + i // (DEC_SEQ // tm)

    naq, st_k, st_v, st_ckv, st_kr, qm, km, vm = _projection(xp, mod3, w, rope=False, state=True)
    a_na, a_mla = _ctx_attention(naq, st_k, st_v, qm, km, vm)
    x1 = _out_projection(a_na, a_mla, xp, mod3, ctx_mod, w_o_b[:NA_WIDTH], w_o_b[NA_WIDTH:], g1, b1)
    yp = _ffn(x1, mod3, ctx_mod, w_gu_b, w_down_b, g2, b2)

    naq, nak, nav, qm, km, vm = _projection(xs, mod3, w, rope=True, state=False, cos=cos, sin=sin)
    kr_cache = jnp.pad(cache_mla_krope[:, 0].reshape(DEC_BATCH * PAST_LEN, MLA_ROPE_DIM),
                       ((0, 0), (0, LANES - MLA_ROPE_DIM)))
    kmc, vmc = _cache_kv(cache_mla_ckv[:, 0].reshape(DEC_BATCH * PAST_LEN, KV_LORA), kr_cache, w)
    kc = cache_na_k[:, 0].reshape(DEC_BATCH * PAST_LEN, NA_WIDTH).astype(BF)
    vc = cache_na_v[:, 0].reshape(DEC_BATCH * PAST_LEN, NA_WIDTH).astype(BF)
    a_na = _lat_na_attention(naq, nak, nav, kc, vc, _na_bias_tables(na_rpb[0]))
    a_mla = _lat_mla_attention(qm, km, vm, kmc, vmc)
    x1 = _out_projection(a_na, a_mla, xs, mod3, lat_mod, w_o_b[:NA_WIDTH], w_o_b[NA_WIDTH:], g1, b1)
    ys = _ffn(x1, mod3, lat_mod, w_gu_b, w_down_b, g2, b2)

    return (yp.reshape(BATCH, SEQ, D_MODEL),
            ys.reshape(DEC_BATCH, DEC_SEQ, D_MODEL),
            st_k.reshape(BATCH, 1, SEQ, NA_HEADS, NA_HEAD_DIM),
            st_v.reshape(BATCH, 1, SEQ, NA_HEADS, NA_HEAD_DIM),
            st_ckv.reshape(BATCH, 1, SEQ, KV_LORA),
            st_kr.reshape(BATCH, 1, SEQ, MLA_ROPE_DIM))
```

```python
import functools

import numpy as np
import jax
import jax.numpy as jnp
from jax import lax
from jax.experimental import pallas as pl
from jax.experimental.pallas import tpu as pltpu

D_MODEL = 2048
BATCH = 32
SEQ = 256
DEC_BATCH = 2
DEC_SEQ = 4096
PAST_LEN = 512
GRID_W = 64
GRID_H = DEC_SEQ // GRID_W
NA_HEADS = 8
NA_HEAD_DIM = 128
NA_WIN_ROWS = 8
NA_WIN_COLS = 16
MLA_HEADS = 8
MLA_NOPE_DIM = 128
MLA_ROPE_DIM = 64
MLA_V_DIM = 128
Q_LORA = 512
KV_LORA = 512
NA_WIDTH = NA_HEADS * NA_HEAD_DIM
MLA_WIDTH = MLA_HEADS * MLA_V_DIM
D_FF = -(-8 * D_MODEL // (3 * 256)) * 256
ROPE_THETA = 10000.0
LN_EPS = 1e-5
RMS_EPS = 1e-6
ALPHA = 2.0 ** 0.25
LOG2E = 1.4426950408889634
NA_SCALE = NA_HEAD_DIM ** -0.5 * LOG2E
MLA_SCALE = (MLA_NOPE_DIM + MLA_ROPE_DIM) ** -0.5 * LOG2E

BF = jnp.bfloat16
F32 = jnp.float32
LANES = 128
MLA_QK_PAD = 2 * LANES
NEG = -1e30
VMEM_LIMIT = 60 * 1024 * 1024

N_TOK = BATCH * SEQ
assert N_TOK == DEC_BATCH * DEC_SEQ

PROJ_TM = 512
PROJ_TN = 512
OPROJ_TM = 512
FFN_TM = 1024
FFN_TF = 512
FFN_TN = 512
MLA_TQ = 1024
MLA_KB = 512
MLA_HPS = 2
NA_QROWS = 4
NA_STEPS = GRID_H // NA_QROWS
NA_KROWS = NA_QROWS + NA_WIN_ROWS
NA_QTOK = NA_QROWS * GRID_W
NA_WIN_TOK = NA_KROWS * GRID_W
NA_VARIANT_STEPS = (0, 1, NA_STEPS - 1)
assert NA_QROWS == NA_WIN_ROWS // 2 and NA_KROWS % 2 == 0


def _cparams(sem):
    return pltpu.CompilerParams(dimension_semantics=sem, vmem_limit_bytes=VMEM_LIMIT)


def _dot(a, b):
    return jnp.dot(a, b, preferred_element_type=F32)


def _dot_nt(a, b):
    return lax.dot_general(a, b, (((1,), (1,)), ((), ())), preferred_element_type=F32)


def _sigmoid(x):
    return 1.0 / (1.0 + jnp.exp(-x))


def _layernorm(y, g, b):
    mu = jnp.mean(y, axis=-1, keepdims=True)
    yc = y - mu
    var = jnp.mean(yc * yc, axis=-1, keepdims=True)
    return yc * lax.rsqrt(var + LN_EPS) * g + b


def _rmsnorm(x, g):
    return x * lax.rsqrt(jnp.mean(x * x, axis=-1, keepdims=True) + RMS_EPS) * g


def _mod_kernel(c_ref, w_ref, b_ref, o_ref):
    c = c_ref[...]
    s = (c * _sigmoid(c)).astype(BF)
    o_ref[...] = _dot(s, w_ref[...].astype(BF)) + b_ref[...]


def _modulation(cond8, w_mod, b_mod):
    tn = 1024
    n = 6 * D_MODEL
    return pl.pallas_call(
        _mod_kernel,
        out_shape=jax.ShapeDtypeStruct((8, n), F32),
        grid=(n // tn,),
        in_specs=[pl.BlockSpec((8, D_MODEL), lambda j: (0, 0)),
                  pl.BlockSpec((D_MODEL, tn), lambda j: (0, j)),
                  pl.BlockSpec((1, tn), lambda j: (0, j))],
        out_specs=pl.BlockSpec((8, tn), lambda j: (0, j)),
        compiler_params=_cparams(("parallel",)),
        name="modulation",
    )(cond8, w_mod, b_mod)


def _rope128(x, cos, sin):
    lane = lax.broadcasted_iota(jnp.int32, x.shape, 1)
    partner = jnp.where(lane % 32 < 16, pltpu.roll(x, LANES - 16, 1), pltpu.roll(x, 16, 1))
    return x * cos + partner * sin


def _proj_kernel(*refs, rope, state):
    it = iter(refs)
    x_ref, mod_ref, wm_ref, wr_ref, qg_ref, kvg_ref = (next(it) for _ in range(6))
    wqn_ref, wqr_ref, wkn_ref, wv_ref = (next(it) for _ in range(4))
    cos_ref, sin_ref = (next(it), next(it)) if rope else (None, None)
    naq_ref, nak_ref, nav_ref = (next(it) for _ in range(3))
    ckv_ref, kr_ref = (next(it), next(it)) if state else (None, None)
    qm_ref, km_ref, vm_ref, h_scr, kr_scr = (next(it) for _ in range(5))

    j = pl.program_id(1)

    @pl.when(j == 0)
    def _():
        h = x_ref[...] * (1.0 + mod_ref[0, 1:2, :]) + mod_ref[0, 0:1, :]
        hb = h.astype(BF)
        h_scr[...] = hb
        kr = _dot(hb, wr_ref[...])
        if state:
            kr_ref[...] = kr[:, :MLA_ROPE_DIM]
        if rope:
            kr = _rope128(kr, cos_ref[...], sin_ref[...])
        kr_scr[...] = kr

    z = _dot(h_scr[...], wm_ref[...])

    @pl.when(j < 2)
    def _():
        naq_ref[...] = (z * NA_SCALE).astype(naq_ref.dtype)

    @pl.when((j >= 2) & (j < 4))
    def _():
        nak_ref[...] = z.astype(nak_ref.dtype)

    @pl.when((j >= 4) & (j < 6))
    def _():
        nav_ref[...] = z.astype(nav_ref.dtype)

    @pl.when(j == 6)
    def _():
        qn = _rmsnorm(z, qg_ref[...]).astype(BF)
        qnope = _dot(qn, wqn_ref[...]) * MLA_SCALE
        qrope = _dot(qn, wqr_ref[...]) * MLA_SCALE
        for h in range(MLA_HEADS):
            sl = slice(h * LANES, (h + 1) * LANES)
            r = qrope[:, sl]
            if rope:
                r = _rope128(r, cos_ref[...], sin_ref[...])
            qm_ref[:, h * MLA_QK_PAD: h * MLA_QK_PAD + LANES] = qnope[:, sl].astype(BF)
            qm_ref[:, h * MLA_QK_PAD + LANES: (h + 1) * MLA_QK_PAD] = r.astype(BF)

    @pl.when(j == 7)
    def _():
        ckv = _rmsnorm(z, kvg_ref[...])
        if state:
            ckv_ref[...] = ckv
        cb = ckv.astype(BF)
        kn = _dot(cb, wkn_ref[...])
        vm_ref[...] = _dot(cb, wv_ref[...]).astype(BF)
        krb = kr_scr[...].astype(BF)
        for h in range(MLA_HEADS):
            km_ref[:, h * MLA_QK_PAD: h * MLA_QK_PAD + LANES] = kn[:, h * LANES:(h + 1) * LANES].astype(BF)
            km_ref[:, h * MLA_QK_PAD + LANES: (h + 1) * MLA_QK_PAD] = krb


def _projection(x2d, mod3, w, *, rope, state, cos=None, sin=None):
    tm, tn = PROJ_TM, PROJ_TN
    n_i = N_TOK // tm
    tiles_per_batch = DEC_SEQ // tm
    if rope:
        mod_map = lambda i, j: (1 + i // tiles_per_batch, 0, 0)
    else:
        mod_map = lambda i, j: (0, 0, 0)
    const = lambda i, j: (0, 0)
    row = lambda i, j: (i, 0)
    in_specs = [
        pl.BlockSpec((tm, D_MODEL), row),
        pl.BlockSpec((1, 6, D_MODEL), mod_map),
        pl.BlockSpec((D_MODEL, tn), lambda i, j: (0, j)),
        pl.BlockSpec((D_MODEL, LANES), const),
        pl.BlockSpec((1, Q_LORA), const),
        pl.BlockSpec((1, KV_LORA), const),
        pl.BlockSpec((Q_LORA, NA_WIDTH), const),
        pl.BlockSpec((Q_LORA, NA_WIDTH), const),
        pl.BlockSpec((KV_LORA, NA_WIDTH), const),
        pl.BlockSpec((KV_LORA, MLA_WIDTH), const),
    ]
    args = [x2d, mod3, w["w_main"], w["w_rope"], w["q_g"], w["kv_g"], w["w_qn"], w["w_qr"], w["w_kn"], w["w_v"]]
    if rope:
        in_specs += [pl.BlockSpec((tm, LANES), lambda i, j: (i % tiles_per_batch, 0))] * 2
        args += [cos, sin]
    kv_dtype = F32 if state else BF
    out_shape = [jax.ShapeDtypeStruct((N_TOK, NA_WIDTH), BF),
                 jax.ShapeDtypeStruct((N_TOK, NA_WIDTH), kv_dtype),
                 jax.ShapeDtypeStruct((N_TOK, NA_WIDTH), kv_dtype)]
    out_specs = [pl.BlockSpec((tm, tn), lambda i, j: (i, jnp.minimum(j, 1))),
                 pl.BlockSpec((tm, tn), lambda i, j: (i, jnp.clip(j - 2, 0, 1))),
                 pl.BlockSpec((tm, tn), lambda i, j: (i, jnp.clip(j - 4, 0, 1)))]
    if state:
        out_shape += [jax.ShapeDtypeStruct((N_TOK, KV_LORA), F32),
                      jax.ShapeDtypeStruct((N_TOK, MLA_ROPE_DIM), F32)]
        out_specs += [pl.BlockSpec((tm, KV_LORA), row), pl.BlockSpec((tm, MLA_ROPE_DIM), row)]
    out_shape += [jax.ShapeDtypeStruct((N_TOK, MLA_HEADS * MLA_QK_PAD), BF),
                  jax.ShapeDtypeStruct((N_TOK, MLA_HEADS * MLA_QK_PAD), BF),
                  jax.ShapeDtypeStruct((N_TOK, MLA_WIDTH), BF)]
    out_specs += [pl.BlockSpec((tm, MLA_HEADS * MLA_QK_PAD), row),
                  pl.BlockSpec((tm, MLA_HEADS * MLA_QK_PAD), row),
                  pl.BlockSpec((tm, MLA_WIDTH), row)]
    return pl.pallas_call(
        functools.partial(_proj_kernel, rope=rope, state=state),
        out_shape=out_shape,
        grid=(n_i, 4096 // tn),
        in_specs=in_specs,
        out_specs=out_specs,
        scratch_shapes=[pltpu.VMEM((tm, D_MODEL), BF), pltpu.VMEM((tm, LANES), F32)],
        compiler_params=_cparams(("parallel", "arbitrary")),
        name="projection_latent" if rope else "projection_context",
    )(*args)


def _cache_kv_kernel(ckv_ref, kr_ref, wkn_ref, wv_ref, km_ref, vm_ref):
    cb = ckv_ref[...].astype(BF)
    kn = _dot(cb, wkn_ref[...])
    vm_ref[...] = _dot(cb, wv_ref[...]).astype(BF)
    krb = kr_ref[...].astype(BF)
    for h in range(MLA_HEADS):
        km_ref[:, h * MLA_QK_PAD: h * MLA_QK_PAD + LANES] = kn[:, h * LANES:(h + 1) * LANES].astype(BF)
        km_ref[:, h * MLA_QK_PAD + LANES: (h + 1) * MLA_QK_PAD] = krb


def _cache_kv(ckv2d, kr2d, w):
    n = DEC_BATCH * PAST_LEN
    tm = PAST_LEN
    const = lambda i: (0, 0)
    row = lambda i: (i, 0)
    return pl.pallas_call(
        _cache_kv_kernel,
        out_shape=[jax.ShapeDtypeStruct((n, MLA_HEADS * MLA_QK_PAD), BF),
                   jax.ShapeDtypeStruct((n, MLA_WIDTH), BF)],
        grid=(n // tm,),
        in_specs=[pl.BlockSpec((tm, KV_LORA), row), pl.BlockSpec((tm, LANES), row),
                  pl.BlockSpec((KV_LORA, NA_WIDTH), const), pl.BlockSpec((KV_LORA, MLA_WIDTH), const)],
        out_specs=[pl.BlockSpec((tm, MLA_HEADS * MLA_QK_PAD), row), pl.BlockSpec((tm, MLA_WIDTH), row)],
        compiler_params=_cparams(("parallel",)),
        name="cache_kv_expand",
    )(ckv2d, kr2d, w["w_kn"], w["w_v"])


def _softmax_av(s_list, v_list):
    m = functools.reduce(jnp.maximum, [s.max(axis=-1, keepdims=True) for s in s_list])
    p_list = [jnp.exp2(s - m) for s in s_list]
    l = functools.reduce(jnp.add, [p.sum(axis=-1, keepdims=True) for p in p_list])
    o = functools.reduce(jnp.add, [_dot(p.astype(BF), v) for p, v in zip(p_list, v_list)])
    return o * (1.0 / l)


def _ctx_attn_kernel(q_ref, k_ref, v_ref, qm_ref, km_ref, vm_ref, ona_ref, omla_ref):
    for h in range(NA_HEADS):
        sl = slice(h * NA_HEAD_DIM, (h + 1) * NA_HEAD_DIM)
        s = _dot_nt(q_ref[:, sl], k_ref[:, sl].astype(BF))
        ona_ref[:, sl] = _softmax_av([s], [v_ref[:, sl].astype(BF)]).astype(BF)
    for h in range(MLA_HEADS):
        sq = slice(h * MLA_QK_PAD, (h + 1) * MLA_QK_PAD)
        sv = slice(h * MLA_V_DIM, (h + 1) * MLA_V_DIM)
        s = _dot_nt(qm_ref[:, sq], km_ref[:, sq])
        omla_ref[:, sv] = _softmax_av([s], [vm_ref[:, sv]]).astype(BF)


def _ctx_attention(naq, nak, nav, qm, km, vm):
    row = lambda b: (b, 0)
    wide = MLA_HEADS * MLA_QK_PAD
    return pl.pallas_call(
        _ctx_attn_kernel,
        out_shape=[jax.ShapeDtypeStruct((N_TOK, NA_WIDTH), BF), jax.ShapeDtypeStruct((N_TOK, MLA_WIDTH), BF)],
        grid=(BATCH,),
        in_specs=[pl.BlockSpec((SEQ, NA_WIDTH), row)] * 3
                 + [pl.BlockSpec((SEQ, wide), row)] * 2 + [pl.BlockSpec((SEQ, MLA_WIDTH), row)],
        out_specs=[pl.BlockSpec((SEQ, NA_WIDTH), row), pl.BlockSpec((SEQ, MLA_WIDTH), row)],
        compiler_params=_cparams(("parallel",)),
        name="context_attention",
    )(naq, nak, nav, qm, km, vm)


def _na_window_start(step):
    return jnp.clip(NA_QROWS * step - NA_WIN_ROWS // 2, 0, GRID_H - NA_KROWS)


def _lat_na_kernel(q_ref, k_ref, v_ref, kc_ref, vc_ref, bias_ref, o_ref):
    start = pl.multiple_of(_na_window_start(pl.program_id(1)) * GRID_W, NA_QTOK)
    for h in range(NA_HEADS):
        sl = slice(h * NA_HEAD_DIM, (h + 1) * NA_HEAD_DIM)
        q = q_ref[:, sl]
        s_loc = _dot_nt(q, k_ref[pl.ds(start, NA_WIN_TOK), sl]) + bias_ref[0, h]
        s_ctx = _dot_nt(q, kc_ref[:, sl])
        o = _softmax_av([s_loc, s_ctx], [v_ref[pl.ds(start, NA_WIN_TOK), sl], vc_ref[:, sl]])
        o_ref[:, sl] = o.astype(BF)


def _na_bias_variant(step):
    return jnp.where(step == 0, 0, jnp.where(step == NA_STEPS - 1, 2, 1))


def _lat_na_attention(naq, nak, nav, kc, vc, bias):
    whole = pl.Buffered(1)
    return pl.pallas_call(
        _lat_na_kernel,
        out_shape=jax.ShapeDtypeStruct((N_TOK, NA_WIDTH), BF),
        grid=(DEC_BATCH, NA_STEPS),
        in_specs=[pl.BlockSpec((NA_QTOK, NA_WIDTH), lambda b, j: (b * NA_STEPS + j, 0)),
                  pl.BlockSpec((DEC_SEQ, NA_WIDTH), lambda b, j: (b, 0), pipeline_mode=whole),
                  pl.BlockSpec((DEC_SEQ, NA_WIDTH), lambda b, j: (b, 0), pipeline_mode=whole),
                  pl.BlockSpec((PAST_LEN, NA_WIDTH), lambda b, j: (b, 0)),
                  pl.BlockSpec((PAST_LEN, NA_WIDTH), lambda b, j: (b, 0)),
                  pl.BlockSpec((1, NA_HEADS, NA_QTOK, NA_WIN_TOK), lambda b, j: (_na_bias_variant(j), 0, 0, 0))],
        out_specs=pl.BlockSpec((NA_QTOK, NA_WIDTH), lambda b, j: (b * NA_STEPS + j, 0)),
        compiler_params=_cparams(("parallel", "arbitrary")),
        name="latent_neighbourhood_attention",
    )(naq, nak, nav, kc, vc, bias)


def _lat_mla_kernel(q_ref, k_ref, v_ref, kc_ref, vc_ref, o_ref):
    heads = range(MLA_HPS)
    qk = [slice(h * MLA_QK_PAD, (h + 1) * MLA_QK_PAD) for h in heads]
    vv = [slice(h * MLA_V_DIM, (h + 1) * MLA_V_DIM) for h in heads]
    q = [q_ref[:, qk[h]] for h in heads]
    m, l, acc = [], [], []
    for h in heads:
        s = _dot_nt(q[h], kc_ref[:, qk[h]])
        m.append(s.max(axis=-1, keepdims=True))
        p = jnp.exp2(s - m[h])
        l.append(p.sum(axis=-1, keepdims=True))
        acc.append(_dot(p.astype(BF), vc_ref[:, vv[h]]))
    for c in range(DEC_SEQ // MLA_KB):
        ks = slice(c * MLA_KB, (c + 1) * MLA_KB)
        for h in heads:
            s = _dot_nt(q[h], k_ref[ks, qk[h]])
            m_new = jnp.maximum(m[h], s.max(axis=-1, keepdims=True))
            a = jnp.exp2(m[h] - m_new)
            p = jnp.exp2(s - m_new)
            l[h] = a * l[h] + p.sum(axis=-1, keepdims=True)
            acc[h] = a * acc[h] + _dot(p.astype(BF), v_ref[ks, vv[h]])
            m[h] = m_new
    for h in heads:
        o_ref[:, vv[h]] = (acc[h] * (1.0 / l[h])).astype(BF)


def _lat_mla_attention(qm, km, vm, kmc, vmc):
    nq = DEC_SEQ // MLA_TQ
    qw, vw = MLA_HPS * MLA_QK_PAD, MLA_HPS * MLA_V_DIM
    return pl.pallas_call(
        _lat_mla_kernel,
        out_shape=jax.ShapeDtypeStruct((N_TOK, MLA_WIDTH), BF),
        grid=(DEC_BATCH, MLA_HEADS // MLA_HPS, nq),
        in_specs=[pl.BlockSpec((MLA_TQ, qw), lambda b, h, i: (b * nq + i, h)),
                  pl.BlockSpec((DEC_SEQ, qw), lambda b, h, i: (b, h)),
                  pl.BlockSpec((DEC_SEQ, vw), lambda b, h, i: (b, h)),
                  pl.BlockSpec((PAST_LEN, qw), lambda b, h, i: (b, h)),
                  pl.BlockSpec((PAST_LEN, vw), lambda b, h, i: (b, h))],
        out_specs=pl.BlockSpec((MLA_TQ, vw), lambda b, h, i: (b * nq + i, h)),
        compiler_params=_cparams(("parallel", "parallel", "arbitrary")),
        name="latent_mla_attention",
    )(qm, km, vm, kmc, vmc)


def _oproj_kernel(ana_ref, amla_ref, x_ref, mod_ref, w1_ref, w2_ref, g_ref, b_ref, o_ref):
    o = _dot(ana_ref[...], w1_ref[...]) + _dot(amla_ref[...], w2_ref[...])
    y = ALPHA * x_ref[...] + mod_ref[0, 2:3, :] * o
    o_ref[...] = _layernorm(y, g_ref[...], b_ref[...])


def _out_projection(ana, amla, x2d, mod3, mod_map1, w_o_na, w_o_mla, g, b):
    tm = OPROJ_TM
    row = lambda i: (i, 0)
    const = lambda i: (0, 0)
    whole = pl.Buffered(1)
    return pl.pallas_call(
        _oproj_kernel,
        out_shape=jax.ShapeDtypeStruct((N_TOK, D_MODEL), F32),
        grid=(N_TOK // tm,),
        in_specs=[pl.BlockSpec((tm, NA_WIDTH), row), pl.BlockSpec((tm, MLA_WIDTH), row),
                  pl.BlockSpec((tm, D_MODEL), row),
                  pl.BlockSpec((1, 6, D_MODEL), lambda i: (mod_map1(i, tm), 0, 0)),
                  pl.BlockSpec((NA_WIDTH, D_MODEL), const, pipeline_mode=whole),
                  pl.BlockSpec((MLA_WIDTH, D_MODEL), const, pipeline_mode=whole),
                  pl.BlockSpec((1, D_MODEL), const), pl.BlockSpec((1, D_MODEL), const)],
        out_specs=pl.BlockSpec((tm, D_MODEL), row),
        compiler_params=_cparams(("parallel",)),
        name="out_projection_ln1",
    )(ana, amla, x2d, mod3, w_o_na, w_o_mla, g, b)


def _ffn_kernel(x_ref, mod_ref, wg_ref, wu_ref, wd_ref, g_ref, b_ref, o_ref, h_scr):
    j = pl.program_id(1)

    @pl.when(j == 0)
    def _():
        h_scr[...] = (x_ref[...] * (1.0 + mod_ref[0, 4:5, :]) + mod_ref[0, 3:4, :]).astype(BF)
        o_ref[...] = jnp.zeros_like(o_ref)

    hb = h_scr[...]
    gate = _dot(hb, wg_ref[...])
    up = _dot(hb, wu_ref[...])
    act = (gate * _sigmoid(gate) * up).astype(BF)
    for n in range(D_MODEL // FFN_TN):
        sl = slice(n * FFN_TN, (n + 1) * FFN_TN)
        o_ref[:, sl] += _dot(act, wd_ref[:, sl])

    @pl.when(j == pl.num_programs(1) - 1)
    def _():
        y = ALPHA * x_ref[...] + mod_ref[0, 5:6, :] * o_ref[...]
        o_ref[...] = _layernorm(y, g_ref[...], b_ref[...])


def _ffn(x1, mod3, mod_map1, w_gu, w_down, g, b):
    tm, tf = FFN_TM, FFN_TF
    nf = D_FF // tf
    const = lambda i, j: (0, 0)
    row = lambda i, j: (i, 0)
    return pl.pallas_call(
        _ffn_kernel,
        out_shape=jax.ShapeDtypeStruct((N_TOK, D_MODEL), F32),
        grid=(N_TOK // tm, nf),
        in_specs=[pl.BlockSpec((tm, D_MODEL), row, pipeline_mode=pl.Buffered(1)),
                  pl.BlockSpec((1, 6, D_MODEL), lambda i, j: (mod_map1(i, tm), 0, 0)),
                  pl.BlockSpec((D_MODEL, tf), lambda i, j: (0, j)),
                  pl.BlockSpec((D_MODEL, tf), lambda i, j: (0, j + nf)),
                  pl.BlockSpec((tf, D_MODEL), lambda i, j: (j, 0)),
                  pl.BlockSpec((1, D_MODEL), const), pl.BlockSpec((1, D_MODEL), const)],
        out_specs=pl.BlockSpec((tm, D_MODEL), row),
        scratch_shapes=[pltpu.VMEM((tm, D_MODEL), BF)],
        compiler_params=_cparams(("parallel", "arbitrary")),
        name="swiglu_ln2",
    )(x1, mod3, w_gu, w_gu, w_down, g, b)


def _prep_weights(w_in, q_a_norm, kv_a_norm, w_q_b, w_kv_b):
    w_in_b = w_in.astype(BF)
    wq = w_q_b.astype(BF).reshape(Q_LORA, MLA_HEADS, MLA_NOPE_DIM + MLA_ROPE_DIM)
    wkv = w_kv_b.astype(BF).reshape(KV_LORA, MLA_HEADS, MLA_NOPE_DIM + MLA_V_DIM)
    pad = LANES - MLA_ROPE_DIM
    return {
        "w_main": w_in_b[:, :4096],
        "w_rope": jnp.pad(w_in_b[:, 4096:], ((0, 0), (0, pad))),
        "q_g": q_a_norm.reshape(1, Q_LORA),
        "kv_g": kv_a_norm.reshape(1, KV_LORA),
        "w_qn": wq[:, :, :MLA_NOPE_DIM].reshape(Q_LORA, NA_WIDTH),
        "w_qr": jnp.pad(wq[:, :, MLA_NOPE_DIM:], ((0, 0), (0, 0), (0, pad))).reshape(Q_LORA, NA_WIDTH),
        "w_kn": wkv[:, :, :MLA_NOPE_DIM].reshape(KV_LORA, NA_WIDTH),
        "w_v": wkv[:, :, MLA_NOPE_DIM:].reshape(KV_LORA, MLA_WIDTH),
    }


def _rope_tables():
    half = MLA_ROPE_DIM // 2
    inv_freq = ROPE_THETA ** (-jnp.arange(0, half, 2, dtype=F32) / half)
    t = jnp.arange(DEC_SEQ, dtype=jnp.int32)

    def tables(pos):
        ang = pos.astype(F32)[:, None] * inv_freq
        s = jnp.sin(ang)
        return jnp.concatenate([jnp.cos(ang)] * 2, -1), jnp.concatenate([-s, s], -1)

    cr, sr = tables(t // GRID_W)
    cc, sc = tables(t % GRID_W)
    pad = LANES - MLA_ROPE_DIM
    cos = jnp.concatenate([cr, cc, jnp.ones((DEC_SEQ, pad), F32)], -1)
    sin = jnp.concatenate([sr, sc, jnp.zeros((DEC_SEQ, pad), F32)], -1)
    return cos, sin


def _na_bias_plan():
    plan = []
    for step in NA_VARIANT_STEPS:
        start = min(max(NA_QROWS * step - NA_WIN_ROWS // 2, 0), GRID_H - NA_KROWS)
        per_row = []
        for i in range(NA_QROWS):
            r = NA_QROWS * step + i
            rs = min(max(r - NA_WIN_ROWS // 2, 0), GRID_H - NA_WIN_ROWS)
            per_row.append([(start + u - r + NA_WIN_ROWS - 1, rs <= start + u < rs + NA_WIN_ROWS)
                            for u in range(NA_KROWS)])
        plan.append(per_row)
    return plan


def _na_bias_kernel(ra_ref, rb_ref, o_ref):
    shape = (GRID_W, LANES)
    lane = lax.broadcasted_iota(jnp.int32, shape, 1)
    qc = lax.broadcasted_iota(jnp.int32, shape, 0)
    kc = lane % GRID_W
    cs = jnp.clip(qc - NA_WIN_COLS // 2, 0, GRID_W - NA_WIN_COLS)
    ok_col = (kc >= cs) & (kc < cs + NA_WIN_COLS)
    left = lane < GRID_W
    neg = jnp.full(shape, NEG, F32)
    for v, per_row in enumerate(_na_bias_plan()):
        for i, rows in enumerate(per_row):
            for t in range(len(rows) // 2):
                (dra, oka), (drb, okb) = rows[2 * t], rows[2 * t + 1]
                if oka and okb:
                    ok = ok_col
                elif oka:
                    ok = ok_col & left
                elif okb:
                    ok = ok_col & jnp.logical_not(left)
                for h in range(NA_HEADS):
                    if oka or okb:
                        src = jnp.zeros((1, LANES), F32)
                        if oka:
                            src = src + ra_ref[h, dra:dra + 1, :]
                        if okb:
                            src = src + rb_ref[h, drb:drb + 1, :]
                        val = pltpu.roll(jnp.broadcast_to(src, shape), LANES - (NA_WIN_COLS - 1), 1,
                                         stride=1, stride_axis=0)
                        tile = jnp.where(ok, val * LOG2E, neg)
                    else:
                        tile = neg
                    o_ref[v, h, i * GRID_W:(i + 1) * GRID_W, t * LANES:(t + 1) * LANES] = tile


def _na_bias_tables(rpb):
    n_dr, n_dc = 2 * NA_WIN_ROWS - 1, 2 * NA_WIN_COLS - 1
    ra = jnp.pad(rpb, ((0, 0), (0, 16 - n_dr), (0, LANES - n_dc)))
    rb = jnp.pad(rpb, ((0, 0), (0, 16 - n_dr), (GRID_W, LANES - GRID_W - n_dc)))
    return pl.pallas_call(
        _na_bias_kernel,
        out_shape=jax.ShapeDtypeStruct((len(NA_VARIANT_STEPS), NA_HEADS, NA_QTOK, NA_WIN_TOK), F32),
        compiler_params=pltpu.CompilerParams(vmem_limit_bytes=VMEM_LIMIT),
        name="na_bias_tables",
    )(ra, rb)


def kernel(x_prompt, x_sample, cache_na_k, cache_na_v, cache_mla_ckv, cache_mla_krope, c, c_ctx,
           w_mod, b_mod, w_in, q_a_norm, kv_a_norm, w_q_b, w_kv_b, na_rpb, w_o,
           ln1_g, ln1_b, w_gu, w_down, ln2_g, ln2_b):
    cond8 = jnp.concatenate([c_ctx[None], c, jnp.zeros((8 - 1 - DEC_BATCH, D_MODEL), F32)], 0)
    mod3 = _modulation(cond8, w_mod[0], b_mod[0][None]).reshape(8, 6, D_MODEL)

    w = _prep_weights(w_in[0], q_a_norm[0], kv_a_norm[0], w_q_b[0], w_kv_b[0])
    w_o_b = w_o[0].astype(BF)
    w_gu_b = w_gu[0].astype(BF)
    w_down_b = w_down[0].astype(BF)
    g1, b1 = ln1_g[0][None], ln1_b[0][None]
    g2, b2 = ln2_g[0][None], ln2_b[0][None]
    cos, sin = _rope_tables()

    xp = x_prompt.reshape(N_TOK, D_MODEL)
    xs = x_sample.reshape(N_TOK, D_MODEL)
    ctx_mod = lambda i, tm: 0
    lat_mod = lambda i, tm: 1 + i // (DEC_SEQ // tm)

    naq, st_k, st_v, st_ckv, st_kr, qm, km, vm = _projection(xp, mod3, w, rope=False, state=True)
    a_na, a_mla = _ctx_attention(naq, st_k, st_v, qm, km, vm)
    x1 = _out_projection(a_na, a_mla, xp, mod3, ctx_mod, w_o_b[:NA_WIDTH], w_o_b[NA_WIDTH:], g1, b1)
    yp = _ffn(x1, mod3, ctx_mod, w_gu_b, w_down_b, g2, b2)

    naq, nak, nav, qm, km, vm = _projection(xs, mod3, w, rope=True, state=False, cos=cos, sin=sin)
    kr_cache = jnp.pad(cache_mla_krope[:, 0].reshape(DEC_BATCH * PAST_LEN, MLA_ROPE_DIM),
                       ((0, 0), (0, LANES - MLA_ROPE_DIM)))
    kmc, vmc = _cache_kv(cache_mla_ckv[:, 0].reshape(DEC_BATCH * PAST_LEN, KV_LORA), kr_cache, w)
    kc = cache_na_k[:, 0].reshape(DEC_BATCH * PAST_LEN, NA_WIDTH).astype(BF)
    vc = cache_na_v[:, 0].reshape(DEC_BATCH * PAST_LEN, NA_WIDTH).astype(BF)
    a_na = _lat_na_attention(naq, nak, nav, kc, vc, _na_bias_tables(na_rpb[0]))
    a_mla = _lat_mla_attention(qm, km, vm, kmc, vmc)
    x1 = _out_projection(a_na, a_mla, xs, mod3, lat_mod, w_o_b[:NA_WIDTH], w_o_b[NA_WIDTH:], g1, b1)
    ys = _ffn(x1, mod3, lat_mod, w_gu_b, w_down_b, g2, b2)

    return (yp.reshape(BATCH, SEQ, D_MODEL),
            ys.reshape(DEC_BATCH, DEC_SEQ, D_MODEL),
            st_k.reshape(BATCH, 1, SEQ, NA_HEADS, NA_HEAD_DIM),
            st_v.reshape(BATCH, 1, SEQ, NA_HEADS, NA_HEAD_DIM),
            st_ckv.reshape(BATCH, 1, SEQ, KV_LORA),
            st_kr.reshape(BATCH, 1, SEQ, MLA_ROPE_DIM))
```

```python
import functools

import numpy as np
import jax
import jax.numpy as jnp
from jax import lax
from jax.experimental import pallas as pl
from jax.experimental.pallas import tpu as pltpu

D_MODEL = 2048
BATCH = 32
SEQ = 256
DEC_BATCH = 2
DEC_SEQ = 4096
PAST_LEN = 512
GRID_W = 64
GRID_H = DEC_SEQ // GRID_W
NA_HEADS = 8
NA_HEAD_DIM = 128
NA_WIN_ROWS = 8
NA_WIN_COLS = 16
MLA_HEADS = 8
MLA_NOPE_DIM = 128
MLA_ROPE_DIM = 64
MLA_V_DIM = 128
Q_LORA = 512
KV_LORA = 512
NA_WIDTH = NA_HEADS * NA_HEAD_DIM
MLA_WIDTH = MLA_HEADS * MLA_V_DIM
D_FF = -(-8 * D_MODEL // (3 * 256)) * 256
ROPE_THETA = 10000.0
LN_EPS = 1e-5
RMS_EPS = 1e-6
ALPHA = 2.0 ** 0.25
LOG2E = 1.4426950408889634
NA_SCALE = NA_HEAD_DIM ** -0.5 * LOG2E
MLA_SCALE = (MLA_NOPE_DIM + MLA_ROPE_DIM) ** -0.5 * LOG2E

BF = jnp.bfloat16
F32 = jnp.float32
LANES = 128
MLA_QK_PAD = 2 * LANES
NEG = -1e30
VMEM_LIMIT = 60 * 1024 * 1024

N_TOK = BATCH * SEQ
assert N_TOK == DEC_BATCH * DEC_SEQ

PROJ_TM = 512
EXPAND_TM = 512
OPROJ_TM = 512
FFN_TM = 1024
FFN_TF = 512
FFN_TN = 512
MLA_TQ = 1024
MLA_KB = 512
MLA_HPS = 2
NA_QROWS = 4
NA_STEPS = GRID_H // NA_QROWS
NA_KROWS = NA_QROWS + NA_WIN_ROWS
NA_QTOK = NA_QROWS * GRID_W
NA_WIN_TOK = NA_KROWS * GRID_W
NA_VARIANT_STEPS = (0, 1, NA_STEPS - 1)
assert NA_QROWS == NA_WIN_ROWS // 2 and NA_KROWS % 2 == 0


def _cparams(sem):
    return pltpu.CompilerParams(dimension_semantics=sem, vmem_limit_bytes=VMEM_LIMIT)


def _dot(a, b):
    return jnp.dot(a, b, preferred_element_type=F32)


def _dot_nt(a, b):
    return lax.dot_general(a, b, (((1,), (1,)), ((), ())), preferred_element_type=F32)


def _sigmoid(x):
    return 1.0 / (1.0 + jnp.exp(-x))


def _layernorm(y, g, b):
    mu = jnp.mean(y, axis=-1, keepdims=True)
    yc = y - mu
    var = jnp.mean(yc * yc, axis=-1, keepdims=True)
    return yc * lax.rsqrt(var + LN_EPS) * g + b


def _rmsnorm(x, g):
    return x * lax.rsqrt(jnp.mean(x * x, axis=-1, keepdims=True) + RMS_EPS) * g


def _mod_kernel(c_ref, w_ref, b_ref, o_ref):
    c = c_ref[...]
    s = (c * _sigmoid(c)).astype(BF)
    o_ref[...] = _dot(s, w_ref[...].astype(BF)) + b_ref[...]


def _modulation(cond8, w_mod, b_mod):
    tn = 1024
    n = 6 * D_MODEL
    return pl.pallas_call(
        _mod_kernel,
        out_shape=jax.ShapeDtypeStruct((8, n), F32),
        grid=(n // tn,),
        in_specs=[pl.BlockSpec((8, D_MODEL), lambda j: (0, 0)),
                  pl.BlockSpec((D_MODEL, tn), lambda j: (0, j)),
                  pl.BlockSpec((1, tn), lambda j: (0, j))],
        out_specs=pl.BlockSpec((8, tn), lambda j: (0, j)),
        compiler_params=_cparams(("parallel",)),
        name="modulation",
    )(cond8, w_mod, b_mod)


def _rope128(x, cos, sin):
    lane = lax.broadcasted_iota(jnp.int32, x.shape, 1)
    partner = jnp.where(lane % 32 < 16, pltpu.roll(x, LANES - 16, 1), pltpu.roll(x, 16, 1))
    return x * cos + partner * sin


def _proj_kernel(x_ref, mod_ref, wm_ref, wr_ref, naq_ref, nak_ref, nav_ref, lat_ref, kr_ref):
    hb = (x_ref[...] * (1.0 + mod_ref[0, 1:2, :]) + mod_ref[0, 0:1, :]).astype(BF)
    outs = (naq_ref, nak_ref, nav_ref, lat_ref)
    for n, o_ref in enumerate(outs):
        z = _dot(hb, wm_ref[:, n * NA_WIDTH:(n + 1) * NA_WIDTH])
        if n == 0:
            z = z * NA_SCALE
        o_ref[...] = z.astype(o_ref.dtype)
    kr_ref[...] = _dot(hb, wr_ref[...])


def _expand_kernel(*refs, rope, state):
    it = iter(refs)
    lat_ref, kr_ref, qg_ref, kvg_ref, wqn_ref, wqr_ref, wkn_ref, wv_ref = (next(it) for _ in range(8))
    cos_ref, sin_ref = (next(it), next(it)) if rope else (None, None)
    qm_ref, km_ref, vm_ref = (next(it) for _ in range(3))
    ckv_ref = next(it) if state else None

    qn = _rmsnorm(lat_ref[:, :Q_LORA], qg_ref[...]).astype(BF)
    qnope = _dot(qn, wqn_ref[...]) * MLA_SCALE
    qrope = _dot(qn, wqr_ref[...]) * MLA_SCALE
    for h in range(MLA_HEADS):
        sl = slice(h * LANES, (h + 1) * LANES)
        r = qrope[:, sl]
        if rope:
            r = _rope128(r, cos_ref[...], sin_ref[...])
        qm_ref[:, h * MLA_QK_PAD: h * MLA_QK_PAD + LANES] = qnope[:, sl].astype(BF)
        qm_ref[:, h * MLA_QK_PAD + LANES: (h + 1) * MLA_QK_PAD] = r.astype(BF)

    ckv = _rmsnorm(lat_ref[:, Q_LORA:], kvg_ref[...])
    if state:
        ckv_ref[...] = ckv
    cb = ckv.astype(BF)
    kn = _dot(cb, wkn_ref[...])
    vm_ref[...] = _dot(cb, wv_ref[...]).astype(BF)
    kr = kr_ref[...]
    if rope:
        kr = _rope128(kr, cos_ref[...], sin_ref[...])
    krb = kr.astype(BF)
    for h in range(MLA_HEADS):
        km_ref[:, h * MLA_QK_PAD: h * MLA_QK_PAD + LANES] = kn[:, h * LANES:(h + 1) * LANES].astype(BF)
        km_ref[:, h * MLA_QK_PAD + LANES: (h + 1) * MLA_QK_PAD] = krb


def _projection(x2d, mod3, mod_map1, w, kv_dtype):
    tm = PROJ_TM
    const = lambda i: (0, 0)
    row = lambda i: (i, 0)
    whole = pl.Buffered(1)
    return pl.pallas_call(
        _proj_kernel,
        out_shape=[jax.ShapeDtypeStruct((N_TOK, NA_WIDTH), BF),
                   jax.ShapeDtypeStruct((N_TOK, NA_WIDTH), kv_dtype),
                   jax.ShapeDtypeStruct((N_TOK, NA_WIDTH), kv_dtype),
                   jax.ShapeDtypeStruct((N_TOK, Q_LORA + KV_LORA), F32),
                   jax.ShapeDtypeStruct((N_TOK, LANES), F32)],
        grid=(N_TOK // tm,),
        in_specs=[pl.BlockSpec((tm, D_MODEL), row),
                  pl.BlockSpec((1, 6, D_MODEL), lambda i: (mod_map1(i, tm), 0, 0)),
                  pl.BlockSpec((D_MODEL, 4 * NA_WIDTH), const, pipeline_mode=whole),
                  pl.BlockSpec((D_MODEL, LANES), const, pipeline_mode=whole)],
        out_specs=[pl.BlockSpec((tm, NA_WIDTH), row)] * 3
                  + [pl.BlockSpec((tm, Q_LORA + KV_LORA), row), pl.BlockSpec((tm, LANES), row)],
        compiler_params=_cparams(("parallel",)),
        name="input_projection",
    )(x2d, mod3, w["w_main"], w["w_rope"])


def _expand(lat, kr, w, *, rope, state, cos=None, sin=None):
    tm = EXPAND_TM
    tiles_per_batch = DEC_SEQ // tm
    const = lambda i: (0, 0)
    row = lambda i: (i, 0)
    wide = MLA_HEADS * MLA_QK_PAD
    in_specs = [pl.BlockSpec((tm, Q_LORA + KV_LORA), row), pl.BlockSpec((tm, LANES), row),
                pl.BlockSpec((1, Q_LORA), const), pl.BlockSpec((1, KV_LORA), const),
                pl.BlockSpec((Q_LORA, NA_WIDTH), const), pl.BlockSpec((Q_LORA, NA_WIDTH), const),
                pl.BlockSpec((KV_LORA, NA_WIDTH), const), pl.BlockSpec((KV_LORA, MLA_WIDTH), const)]
    args = [lat, kr, w["q_g"], w["kv_g"], w["w_qn"], w["w_qr"], w["w_kn"], w["w_v"]]
    if rope:
        in_specs += [pl.BlockSpec((tm, LANES), lambda i: (i % tiles_per_batch, 0))] * 2
        args += [cos, sin]
    out_shape = [jax.ShapeDtypeStruct((N_TOK, wide), BF), jax.ShapeDtypeStruct((N_TOK, wide), BF),
                 jax.ShapeDtypeStruct((N_TOK, MLA_WIDTH), BF)]
    out_specs = [pl.BlockSpec((tm, wide), row), pl.BlockSpec((tm, wide), row), pl.BlockSpec((tm, MLA_WIDTH), row)]
    if state:
        out_shape.append(jax.ShapeDtypeStruct((N_TOK, KV_LORA), F32))
        out_specs.append(pl.BlockSpec((tm, KV_LORA), row))
    return pl.pallas_call(
        functools.partial(_expand_kernel, rope=rope, state=state),
        out_shape=out_shape,
        grid=(N_TOK // tm,),
        in_specs=in_specs,
        out_specs=out_specs,
        compiler_params=_cparams(("parallel",)),
        name="lowrank_expand_latent" if rope else "lowrank_expand_context",
    )(*args)


def _cache_kv_kernel(ckv_ref, kr_ref, wkn_ref, wv_ref, km_ref, vm_ref):
    cb = ckv_ref[...].astype(BF)
    kn = _dot(cb, wkn_ref[...])
    vm_ref[...] = _dot(cb, wv_ref[...]).astype(BF)
    krb = kr_ref[...].astype(BF)
    for h in range(MLA_HEADS):
        km_ref[:, h * MLA_QK_PAD: h * MLA_QK_PAD + LANES] = kn[:, h * LANES:(h + 1) * LANES].astype(BF)
        km_ref[:, h * MLA_QK_PAD + LANES: (h + 1) * MLA_QK_PAD] = krb


def _cache_kv(ckv2d, kr2d, w):
    n = DEC_BATCH * PAST_LEN
    tm = PAST_LEN
    const = lambda i: (0, 0)
    row = lambda i: (i, 0)
    return pl.pallas_call(
        _cache_kv_kernel,
        out_shape=[jax.ShapeDtypeStruct((n, MLA_HEADS * MLA_QK_PAD), BF),
                   jax.ShapeDtypeStruct((n, MLA_WIDTH), BF)],
        grid=(n // tm,),
        in_specs=[pl.BlockSpec((tm, KV_LORA), row), pl.BlockSpec((tm, LANES), row),
                  pl.BlockSpec((KV_LORA, NA_WIDTH), const), pl.BlockSpec((KV_LORA, MLA_WIDTH), const)],
        out_specs=[pl.BlockSpec((tm, MLA_HEADS * MLA_QK_PAD), row), pl.BlockSpec((tm, MLA_WIDTH), row)],
        compiler_params=_cparams(("parallel",)),
        name="cache_kv_expand",
    )(ckv2d, kr2d, w["w_kn"], w["w_v"])


def _softmax_av(s_list, v_list):
    m = functools.reduce(jnp.maximum, [s.max(axis=-1, keepdims=True) for s in s_list])
    p_list = [jnp.exp2(s - m) for s in s_list]
    l = functools.reduce(jnp.add, [p.sum(axis=-1, keepdims=True) for p in p_list])
    o = functools.reduce(jnp.add, [_dot(p.astype(BF), v) for p, v in zip(p_list, v_list)])
    return o * (1.0 / l)


def _ctx_attn_kernel(q_ref, k_ref, v_ref, qm_ref, km_ref, vm_ref, ona_ref, omla_ref):
    for h in range(NA_HEADS):
        sl = slice(h * NA_HEAD_DIM, (h + 1) * NA_HEAD_DIM)
        s = _dot_nt(q_ref[:, sl], k_ref[:, sl].astype(BF))
        ona_ref[:, sl] = _softmax_av([s], [v_ref[:, sl].astype(BF)]).astype(BF)
    for h in range(MLA_HEADS):
        sq = slice(h * MLA_QK_PAD, (h + 1) * MLA_QK_PAD)
        sv = slice(h * MLA_V_DIM, (h + 1) * MLA_V_DIM)
        s = _dot_nt(qm_ref[:, sq], km_ref[:, sq])
        omla_ref[:, sv] = _softmax_av([s], [vm_ref[:, sv]]).astype(BF)


def _ctx_attention(naq, nak, nav, qm, km, vm):
    row = lambda b: (b, 0)
    wide = MLA_HEADS * MLA_QK_PAD
    return pl.pallas_call(
        _ctx_attn_kernel,
        out_shape=[jax.ShapeDtypeStruct((N_TOK, NA_WIDTH), BF), jax.ShapeDtypeStruct((N_TOK, MLA_WIDTH), BF)],
        grid=(BATCH,),
        in_specs=[pl.BlockSpec((SEQ, NA_WIDTH), row)] * 3
                 + [pl.BlockSpec((SEQ, wide), row)] * 2 + [pl.BlockSpec((SEQ, MLA_WIDTH), row)],
        out_specs=[pl.BlockSpec((SEQ, NA_WIDTH), row), pl.BlockSpec((SEQ, MLA_WIDTH), row)],
        compiler_params=_cparams(("parallel",)),
        name="context_attention",
    )(naq, nak, nav, qm, km, vm)


def _na_window_start(step):
    return jnp.clip(NA_QROWS * step - NA_WIN_ROWS // 2, 0, GRID_H - NA_KROWS)


def _lat_na_kernel(q_ref, k_ref, v_ref, kc_ref, vc_ref, bias_ref, o_ref):
    start = pl.multiple_of(_na_window_start(pl.program_id(1)) * GRID_W, NA_QTOK)
    for h in range(NA_HEADS):
        sl = slice(h * NA_HEAD_DIM, (h + 1) * NA_HEAD_DIM)
        q = q_ref[:, sl]
        s_loc = _dot_nt(q, k_ref[pl.ds(start, NA_WIN_TOK), sl]) + bias_ref[0, h]
        s_ctx = _dot_nt(q, kc_ref[:, sl])
        o = _softmax_av([s_loc, s_ctx], [v_ref[pl.ds(start, NA_WIN_TOK), sl], vc_ref[:, sl]])
        o_ref[:, sl] = o.astype(BF)


def _na_bias_variant(step):
    return jnp.where(step == 0, 0, jnp.where(step == NA_STEPS - 1, 2, 1))


def _lat_na_attention(naq, nak, nav, kc, vc, bias):
    whole = pl.Buffered(1)
    return pl.pallas_call(
        _lat_na_kernel,
        out_shape=jax.ShapeDtypeStruct((N_TOK, NA_WIDTH), BF),
        grid=(DEC_BATCH, NA_STEPS),
        in_specs=[pl.BlockSpec((NA_QTOK, NA_WIDTH), lambda b, j: (b * NA_STEPS + j, 0)),
                  pl.BlockSpec((DEC_SEQ, NA_WIDTH), lambda b, j: (b, 0), pipeline_mode=whole),
                  pl.BlockSpec((DEC_SEQ, NA_WIDTH), lambda b, j: (b, 0), pipeline_mode=whole),
                  pl.BlockSpec((PAST_LEN, NA_WIDTH), lambda b, j: (b, 0)),
                  pl.BlockSpec((PAST_LEN, NA_WIDTH), lambda b, j: (b, 0)),
                  pl.BlockSpec((1, NA_HEADS, NA_QTOK, NA_WIN_TOK), lambda b, j: (_na_bias_variant(j), 0, 0, 0))],
        out_specs=pl.BlockSpec((NA_QTOK, NA_WIDTH), lambda b, j: (b * NA_STEPS + j, 0)),
        compiler_params=_cparams(("parallel", "arbitrary")),
        name="latent_neighbourhood_attention",
    )(naq, nak, nav, kc, vc, bias)


def _lat_mla_kernel(q_ref, k_ref, v_ref, kc_ref, vc_ref, o_ref):
    heads = range(MLA_HPS)
    qk = [slice(h * MLA_QK_PAD, (h + 1) * MLA_QK_PAD) for h in heads]
    vv = [slice(h * MLA_V_DIM, (h + 1) * MLA_V_DIM) for h in heads]
    q = [q_ref[:, qk[h]] for h in heads]
    m, l, acc = [], [], []
    for h in heads:
        s = _dot_nt(q[h], kc_ref[:, qk[h]])
        m.append(s.max(axis=-1, keepdims=True))
        p = jnp.exp2(s - m[h])
        l.append(p.sum(axis=-1, keepdims=True))
        acc.append(_dot(p.astype(BF), vc_ref[:, vv[h]]))
    for c in range(DEC_SEQ // MLA_KB):
        ks = slice(c * MLA_KB, (c + 1) * MLA_KB)
        for h in heads:
            s = _dot_nt(q[h], k_ref[ks, qk[h]])
            m_new = jnp.maximum(m[h], s.max(axis=-1, keepdims=True))
            a = jnp.exp2(m[h] - m_new)
            p = jnp.exp2(s - m_new)
            l[h] = a * l[h] + p.sum(axis=-1, keepdims=True)
            acc[h] = a * acc[h] + _dot(p.astype(BF), v_ref[ks, vv[h]])
            m[h] = m_new
    for h in heads:
        o_ref[:, vv[h]] = (acc[h] * (1.0 / l[h])).astype(BF)


def _lat_mla_attention(qm, km, vm, kmc, vmc):
    nq = DEC_SEQ // MLA_TQ
    qw, vw = MLA_HPS * MLA_QK_PAD, MLA_HPS * MLA_V_DIM
    return pl.pallas_call(
        _lat_mla_kernel,
        out_shape=jax.ShapeDtypeStruct((N_TOK, MLA_WIDTH), BF),
        grid=(DEC_BATCH, MLA_HEADS // MLA_HPS, nq),
        in_specs=[pl.BlockSpec((MLA_TQ, qw), lambda b, h, i: (b * nq + i, h)),
                  pl.BlockSpec((DEC_SEQ, qw), lambda b, h, i: (b, h)),
                  pl.BlockSpec((DEC_SEQ, vw), lambda b, h, i: (b, h)),
                  pl.BlockSpec((PAST_LEN, qw), lambda b, h, i: (b, h)),
                  pl.BlockSpec((PAST_LEN, vw), lambda b, h, i: (b, h))],
        out_specs=pl.BlockSpec((MLA_TQ, vw), lambda b, h, i: (b * nq + i, h)),
        compiler_params=_cparams(("parallel", "parallel", "arbitrary")),
        name="latent_mla_attention",
    )(qm, km, vm, kmc, vmc)


def _oproj_kernel(ana_ref, amla_ref, x_ref, mod_ref, w1_ref, w2_ref, g_ref, b_ref, o_ref):
    o = _dot(ana_ref[...], w1_ref[...]) + _dot(amla_ref[...], w2_ref[...])
    y = ALPHA * x_ref[...] + mod_ref[0, 2:3, :] * o
    o_ref[...] = _layernorm(y, g_ref[...], b_ref[...])


def _out_projection(ana, amla, x2d, mod3, mod_map1, w_o_na, w_o_mla, g, b):
    tm = OPROJ_TM
    row = lambda i: (i, 0)
    const = lambda i: (0, 0)
    whole = pl.Buffered(1)
    return pl.pallas_call(
        _oproj_kernel,
        out_shape=jax.ShapeDtypeStruct((N_TOK, D_MODEL), F32),
        grid=(N_TOK // tm,),
        in_specs=[pl.BlockSpec((tm, NA_WIDTH), row), pl.BlockSpec((tm, MLA_WIDTH), row),
                  pl.BlockSpec((tm, D_MODEL), row),
                  pl.BlockSpec((1, 6, D_MODEL), lambda i: (mod_map1(i, tm), 0, 0)),
                  pl.BlockSpec((NA_WIDTH, D_MODEL), const, pipeline_mode=whole),
                  pl.BlockSpec((MLA_WIDTH, D_MODEL), const, pipeline_mode=whole),
                  pl.BlockSpec((1, D_MODEL), const), pl.BlockSpec((1, D_MODEL), const)],
        out_specs=pl.BlockSpec((tm, D_MODEL), row),
        compiler_params=_cparams(("parallel",)),
        name="out_projection_ln1",
    )(ana, amla, x2d, mod3, w_o_na, w_o_mla, g, b)


def _ffn_kernel(x_ref, mod_ref, wg_ref, wu_ref, wd_ref, g_ref, b_ref, o_ref, h_scr):
    j = pl.program_id(1)

    @pl.when(j == 0)
    def _():
        h_scr[...] = (x_ref[...] * (1.0 + mod_ref[0, 4:5, :]) + mod_ref[0, 3:4, :]).astype(BF)
        o_ref[...] = jnp.zeros_like(o_ref)

    hb = h_scr[...]
    gate = _dot(hb, wg_ref[...])
    up = _dot(hb, wu_ref[...])
    act = (gate * _sigmoid(gate) * up).astype(BF)
    for n in range(D_MODEL // FFN_TN):
        sl = slice(n * FFN_TN, (n + 1) * FFN_TN)
        o_ref[:, sl] += _dot(act, wd_ref[:, sl])

    @pl.when(j == pl.num_programs(1) - 1)
    def _():
        y = ALPHA * x_ref[...] + mod_ref[0, 5:6, :] * o_ref[...]
        o_ref[...] = _layernorm(y, g_ref[...], b_ref[...])


def _ffn(x1, mod3, mod_map1, w_gu, w_down, g, b):
    tm, tf = FFN_TM, FFN_TF
    nf = D_FF // tf
    const = lambda i, j: (0, 0)
    row = lambda i, j: (i, 0)
    return pl.pallas_call(
        _ffn_kernel,
        out_shape=jax.ShapeDtypeStruct((N_TOK, D_MODEL), F32),
        grid=(N_TOK // tm, nf),
        in_specs=[pl.BlockSpec((tm, D_MODEL), row, pipeline_mode=pl.Buffered(1)),
                  pl.BlockSpec((1, 6, D_MODEL), lambda i, j: (mod_map1(i, tm), 0, 0)),
                  pl.BlockSpec((D_MODEL, tf), lambda i, j: (0, j)),
                  pl.BlockSpec((D_MODEL, tf), lambda i, j: (0, j + nf)),
                  pl.BlockSpec((tf, D_MODEL), lambda i, j: (j, 0)),
                  pl.BlockSpec((1, D_MODEL), const), pl.BlockSpec((1, D_MODEL), const)],
        out_specs=pl.BlockSpec((tm, D_MODEL), row),
        scratch_shapes=[pltpu.VMEM((tm, D_MODEL), BF)],
        compiler_params=_cparams(("parallel", "arbitrary")),
        name="swiglu_ln2",
    )(x1, mod3, w_gu, w_gu, w_down, g, b)


def _prep_weights(w_in, q_a_norm, kv_a_norm, w_q_b, w_kv_b):
    w_in_b = w_in.astype(BF)
    wq = w_q_b.astype(BF).reshape(Q_LORA, MLA_HEADS, MLA_NOPE_DIM + MLA_ROPE_DIM)
    wkv = w_kv_b.astype(BF).reshape(KV_LORA, MLA_HEADS, MLA_NOPE_DIM + MLA_V_DIM)
    pad = LANES - MLA_ROPE_DIM
    return {
        "w_main": w_in_b,
        "w_rope": jnp.pad(w_in_b[:, 4096:], ((0, 0), (0, pad))),
        "q_g": q_a_norm.reshape(1, Q_LORA),
        "kv_g": kv_a_norm.reshape(1, KV_LORA),
        "w_qn": wq[:, :, :MLA_NOPE_DIM].reshape(Q_LORA, NA_WIDTH),
        "w_qr": jnp.pad(wq[:, :, MLA_NOPE_DIM:], ((0, 0), (0, 0), (0, pad))).reshape(Q_LORA, NA_WIDTH),
        "w_kn": wkv[:, :, :MLA_NOPE_DIM].reshape(KV_LORA, NA_WIDTH),
        "w_v": wkv[:, :, MLA_NOPE_DIM:].reshape(KV_LORA, MLA_WIDTH),
    }


def _rope_tables():
    half = MLA_ROPE_DIM // 2
    inv_freq = ROPE_THETA ** (-jnp.arange(0, half, 2, dtype=F32) / half)
    t = jnp.arange(DEC_SEQ, dtype=jnp.int32)

    def tables(pos):
        ang = pos.astype(F32)[:, None] * inv_freq
        s = jnp.sin(ang)
        return jnp.concatenate([jnp.cos(ang)] * 2, -1), jnp.concatenate([-s, s], -1)

    cr, sr = tables(t // GRID_W)
    cc, sc = tables(t % GRID_W)
    pad = LANES - MLA_ROPE_DIM
    cos = jnp.concatenate([cr, cc, jnp.ones((DEC_SEQ, pad), F32)], -1)
    sin = jnp.concatenate([sr, sc, jnp.zeros((DEC_SEQ, pad), F32)], -1)
    return cos, sin


def _na_bias_plan():
    plan = []
    for step in NA_VARIANT_STEPS:
        start = min(max(NA_QROWS * step - NA_WIN_ROWS // 2, 0), GRID_H - NA_KROWS)
        per_row = []
        for i in range(NA_QROWS):
            r = NA_QROWS * step + i
            rs = min(max(r - NA_WIN_ROWS // 2, 0), GRID_H - NA_WIN_ROWS)
            per_row.append([(start + u - r + NA_WIN_ROWS - 1, rs <= start + u < rs + NA_WIN_ROWS)
                            for u in range(NA_KROWS)])
        plan.append(per_row)
    return plan


def _na_bias_kernel(ra_ref, rb_ref, o_ref):
    shape = (GRID_W, LANES)
    lane = lax.broadcasted_iota(jnp.int32, shape, 1)
    qc = lax.broadcasted_iota(jnp.int32, shape, 0)
    kc = lane % GRID_W
    cs = jnp.clip(qc - NA_WIN_COLS // 2, 0, GRID_W - NA_WIN_COLS)
    ok_col = (kc >= cs) & (kc < cs + NA_WIN_COLS)
    left = lane < GRID_W
    neg = jnp.full(shape, NEG, F32)
    for v, per_row in enumerate(_na_bias_plan()):
        for i, rows in enumerate(per_row):
            for t in range(len(rows) // 2):
                (dra, oka), (drb, okb) = rows[2 * t], rows[2 * t + 1]
                if oka and okb:
                    ok = ok_col
                elif oka:
                    ok = ok_col & left
                elif okb:
                    ok = ok_col & jnp.logical_not(left)
                for h in range(NA_HEADS):
                    if oka or okb:
                        src = jnp.zeros((1, LANES), F32)
                        if oka:
                            src = src + ra_ref[h, dra:dra + 1, :]
                        if okb:
                            src = src + rb_ref[h, drb:drb + 1, :]
                        val = pltpu.roll(jnp.broadcast_to(src, shape), LANES - (NA_WIN_COLS - 1), 1,
                                         stride=1, stride_axis=0)
                        tile = jnp.where(ok, val * LOG2E, neg)
                    else:
                        tile = neg
                    o_ref[v, h, i * GRID_W:(i + 1) * GRID_W, t * LANES:(t + 1) * LANES] = tile


def _na_bias_tables(rpb):
    n_dr, n_dc = 2 * NA_WIN_ROWS - 1, 2 * NA_WIN_COLS - 1
    ra = jnp.pad(rpb, ((0, 0), (0, 16 - n_dr), (0, LANES - n_dc)))
    rb = jnp.pad(rpb, ((0, 0), (0, 16 - n_dr), (GRID_W, LANES - GRID_W - n_dc)))
    return pl.pallas_call(
        _na_bias_kernel,
        out_shape=jax.ShapeDtypeStruct((len(NA_VARIANT_STEPS), NA_HEADS, NA_QTOK, NA_WIN_TOK), F32),
        compiler_params=pltpu.CompilerParams(vmem_limit_bytes=VMEM_LIMIT),
        name="na_bias_tables",
    )(ra, rb)


def kernel(x_prompt, x_sample, cache_na_k, cache_na_v, cache_mla_ckv, cache_mla_krope, c, c_ctx,
           w_mod, b_mod, w_in, q_a_norm, kv_a_norm, w_q_b, w_kv_b, na_rpb, w_o,
           ln1_g, ln1_b, w_gu, w_down, ln2_g, ln2_b):
    cond8 = jnp.concatenate([c_ctx[None], c, jnp.zeros((8 - 1 - DEC_BATCH, D_MODEL), F32)], 0)
    mod3 = _modulation(cond8, w_mod[0], b_mod[0][None]).reshape(8, 6, D_MODEL)

    w = _prep_weights(w_in[0], q_a_norm[0], kv_a_norm[0], w_q_b[0], w_kv_b[0])
    w_o_b = w_o[0].astype(BF)
    w_gu_b = w_gu[0].astype(BF)
    w_down_b = w_down[0].astype(BF)
    g1, b1 = ln1_g[0][None], ln1_b[0][None]
    g2, b2 = ln2_g[0][None], ln2_b[0][None]
    cos, sin = _rope_tables()

    xp = x_prompt.reshape(N_TOK, D_MODEL)
    xs = x_sample.reshape(N_TOK, D_MODEL)
    ctx_mod = lambda i, tm: 0
    lat_mod = lambda i, tm: 1 + i // (DEC_SEQ // tm)

    naq, st_k, st_v, lat, kr = _projection(xp, mod3, ctx_mod, w, F32)
    qm, km, vm, st_ckv = _expand(lat, kr, w, rope=False, state=True)
    st_kr = kr[:, :MLA_ROPE_DIM]
    a_na, a_mla = _ctx_attention(naq, st_k, st_v, qm, km, vm)
    x1 = _out_projection(a_na, a_mla, xp, mod3, ctx_mod, w_o_b[:NA_WIDTH], w_o_b[NA_WIDTH:], g1, b1)
    yp = _ffn(x1, mod3, ctx_mod, w_gu_b, w_down_b, g2, b2)

    naq, nak, nav, lat, kr = _projection(xs, mod3, lat_mod, w, BF)
    qm, km, vm = _expand(lat, kr, w, rope=True, state=False, cos=cos, sin=sin)
    kr_cache = jnp.pad(cache_mla_krope[:, 0].reshape(DEC_BATCH * PAST_LEN, MLA_ROPE_DIM),
                       ((0, 0), (0, LANES - MLA_ROPE_DIM)))
    kmc, vmc = _cache_kv(cache_mla_ckv[:, 0].reshape(DEC_BATCH * PAST_LEN, KV_LORA), kr_cache, w)
    kc = cache_na_k[:, 0].reshape(DEC_BATCH * PAST_LEN, NA_WIDTH).astype(BF)
    vc = cache_na_v[:, 0].reshape(DEC_BATCH * PAST_LEN, NA_WIDTH).astype(BF)
    a_na = _lat_na_attention(naq, nak, nav, kc, vc, _na_bias_tables(na_rpb[0]))
    a_mla = _lat_mla_attention(qm, km, vm, kmc, vmc)
    x1 = _out_projection(a_na, a_mla, xs, mod3, lat_mod, w_o_b[:NA_WIDTH], w_o_b[NA_WIDTH:], g1, b1)
    ys = _ffn(x1, mod3, lat_mod, w_gu_b, w_down_b, g2, b2)

    return (yp.reshape(BATCH, SEQ, D_MODEL),
            ys.reshape(DEC_BATCH, DEC_SEQ, D_MODEL),
            st_k.reshape(BATCH, 1, SEQ, NA_HEADS, NA_HEAD_DIM),
            st_v.reshape(BATCH, 1, SEQ, NA_HEADS, NA_HEAD_DIM),
            st_ckv.reshape(BATCH, 1, SEQ, KV_LORA),
            st_kr.reshape(BATCH, 1, SEQ, MLA_ROPE_DIM))
```

```python
import functools

import numpy as np
import jax
import jax.numpy as jnp
from jax import lax
from jax.experimental import pallas as pl
from jax.experimental.pallas import tpu as pltpu

D_MODEL = 2048
BATCH = 32
SEQ = 256
DEC_BATCH = 2
DEC_SEQ = 4096
PAST_LEN = 512
GRID_W = 64
GRID_H = DEC_SEQ // GRID_W
NA_HEADS = 8
NA_HEAD_DIM = 128
NA_WIN_ROWS = 8
NA_WIN_COLS = 16
MLA_HEADS = 8
MLA_NOPE_DIM = 128
MLA_ROPE_DIM = 64
MLA_V_DIM = 128
Q_LORA = 512
KV_LORA = 512
NA_WIDTH = NA_HEADS * NA_HEAD_DIM
MLA_WIDTH = MLA_HEADS * MLA_V_DIM
D_FF = -(-8 * D_MODEL // (3 * 256)) * 256
ROPE_THETA = 10000.0
LN_EPS = 1e-5
RMS_EPS = 1e-6
ALPHA = 2.0 ** 0.25
LOG2E = 1.4426950408889634
NA_SCALE = NA_HEAD_DIM ** -0.5 * LOG2E
MLA_SCALE = (MLA_NOPE_DIM + MLA_ROPE_DIM) ** -0.5 * LOG2E

BF = jnp.bfloat16
F32 = jnp.float32
LANES = 128
MLA_QK_PAD = 2 * LANES
NEG = -1e30
VMEM_LIMIT = 60 * 1024 * 1024

N_TOK = BATCH * SEQ
assert N_TOK == DEC_BATCH * DEC_SEQ

PROJ_TM = 512
EXPAND_TM = 512
OPROJ_TM = 512
FFN_TM = 1024
FFN_TF = 512
FFN_TN = 512
MLA_TQ = 1024
MLA_KB = 512
MLA_HPS = 2
NA_QROWS = 4
NA_STEPS = GRID_H // NA_QROWS
NA_KROWS = NA_QROWS + NA_WIN_ROWS
NA_QTOK = NA_QROWS * GRID_W
NA_WIN_TOK = NA_KROWS * GRID_W
NA_VARIANT_STEPS = (0, 1, NA_STEPS - 1)
assert NA_QROWS == NA_WIN_ROWS // 2 and NA_KROWS % 2 == 0


def _cparams(sem):
    return pltpu.CompilerParams(dimension_semantics=sem, vmem_limit_bytes=VMEM_LIMIT)


def _dot(a, b):
    return jnp.dot(a, b, preferred_element_type=F32)


def _dot_nt(a, b):
    return lax.dot_general(a, b, (((1,), (1,)), ((), ())), preferred_element_type=F32)


def _sigmoid(x):
    return 1.0 / (1.0 + jnp.exp(-x))


def _layernorm(y, g, b):
    mu = jnp.mean(y, axis=-1, keepdims=True)
    yc = y - mu
    var = jnp.mean(yc * yc, axis=-1, keepdims=True)
    return yc * lax.rsqrt(var + LN_EPS) * g + b


def _rmsnorm(x, g):
    return x * lax.rsqrt(jnp.mean(x * x, axis=-1, keepdims=True) + RMS_EPS) * g


def _mod_kernel(c_ref, w_ref, b_ref, o_ref):
    c = c_ref[...]
    s = (c * _sigmoid(c)).astype(BF)
    o_ref[...] = _dot(s, w_ref[...].astype(BF)) + b_ref[...]


def _modulation(cond8, w_mod, b_mod):
    tn = 1024
    n = 6 * D_MODEL
    return pl.pallas_call(
        _mod_kernel,
        out_shape=jax.ShapeDtypeStruct((8, n), F32),
        grid=(n // tn,),
        in_specs=[pl.BlockSpec((8, D_MODEL), lambda j: (0, 0)),
                  pl.BlockSpec((D_MODEL, tn), lambda j: (0, j)),
                  pl.BlockSpec((1, tn), lambda j: (0, j))],
        out_specs=pl.BlockSpec((8, tn), lambda j: (0, j)),
        compiler_params=_cparams(("parallel",)),
        name="modulation",
    )(cond8, w_mod, b_mod)


def _rope128(x, cos, sin):
    lane = lax.broadcasted_iota(jnp.int32, x.shape, 1)
    partner = jnp.where(lane % 32 < 16, pltpu.roll(x, LANES - 16, 1), pltpu.roll(x, 16, 1))
    return x * cos + partner * sin


def _proj_kernel(x_ref, mod_ref, wm_ref, wr_ref, naq_ref, nak_ref, nav_ref, lat_ref, kr_ref, *state_refs):
    hb = (x_ref[...] * (1.0 + mod_ref[0, 1:2, :]) + mod_ref[0, 0:1, :]).astype(BF)
    outs = (naq_ref, nak_ref, nav_ref, lat_ref)
    for n, o_ref in enumerate(outs):
        z = _dot(hb, wm_ref[:, n * NA_WIDTH:(n + 1) * NA_WIDTH])
        if state_refs and n in (1, 2):
            state_refs[n - 1][...] = z.reshape(z.shape[0], NA_HEADS, NA_HEAD_DIM)
        if n == 0:
            z = z * NA_SCALE
        o_ref[...] = z.astype(o_ref.dtype)
    kr_ref[...] = _dot(hb, wr_ref[...])


def _expand_kernel(*refs, rope, state):
    it = iter(refs)
    lat_ref, kr_ref, qg_ref, kvg_ref, wqn_ref, wqr_ref, wkn_ref, wv_ref = (next(it) for _ in range(8))
    cos_ref, sin_ref = (next(it), next(it)) if rope else (None, None)
    qm_ref, km_ref, vm_ref = (next(it) for _ in range(3))
    ckv_ref = next(it) if state else None

    qn = _rmsnorm(lat_ref[:, :Q_LORA], qg_ref[...]).astype(BF)
    qnope = _dot(qn, wqn_ref[...]) * MLA_SCALE
    qrope = _dot(qn, wqr_ref[...]) * MLA_SCALE
    for h in range(MLA_HEADS):
        sl = slice(h * LANES, (h + 1) * LANES)
        r = qrope[:, sl]
        if rope:
            r = _rope128(r, cos_ref[...], sin_ref[...])
        qm_ref[:, h * MLA_QK_PAD: h * MLA_QK_PAD + LANES] = qnope[:, sl].astype(BF)
        qm_ref[:, h * MLA_QK_PAD + LANES: (h + 1) * MLA_QK_PAD] = r.astype(BF)

    ckv = _rmsnorm(lat_ref[:, Q_LORA:], kvg_ref[...])
    if state:
        ckv_ref[...] = ckv
    cb = ckv.astype(BF)
    kn = _dot(cb, wkn_ref[...])
    vm_ref[...] = _dot(cb, wv_ref[...]).astype(BF)
    kr = kr_ref[...]
    if rope:
        kr = _rope128(kr, cos_ref[...], sin_ref[...])
    krb = kr.astype(BF)
    for h in range(MLA_HEADS):
        km_ref[:, h * MLA_QK_PAD: h * MLA_QK_PAD + LANES] = kn[:, h * LANES:(h + 1) * LANES].astype(BF)
        km_ref[:, h * MLA_QK_PAD + LANES: (h + 1) * MLA_QK_PAD] = krb


def _projection(x2d, mod3, mod_map1, w, *, state):
    tm = PROJ_TM
    const = lambda i: (0, 0)
    row = lambda i: (i, 0)
    whole = pl.Buffered(1)
    out_shape = [jax.ShapeDtypeStruct((N_TOK, NA_WIDTH), BF)] * 3 + [
        jax.ShapeDtypeStruct((N_TOK, Q_LORA + KV_LORA), F32), jax.ShapeDtypeStruct((N_TOK, LANES), F32)]
    out_specs = [pl.BlockSpec((tm, NA_WIDTH), row)] * 3 + [
        pl.BlockSpec((tm, Q_LORA + KV_LORA), row), pl.BlockSpec((tm, LANES), row)]
    if state:
        out_shape += [jax.ShapeDtypeStruct((N_TOK, NA_HEADS, NA_HEAD_DIM), F32)] * 2
        out_specs += [pl.BlockSpec((tm, NA_HEADS, NA_HEAD_DIM), lambda i: (i, 0, 0))] * 2
    return pl.pallas_call(
        _proj_kernel,
        out_shape=out_shape,
        grid=(N_TOK // tm,),
        in_specs=[pl.BlockSpec((tm, D_MODEL), row),
                  pl.BlockSpec((1, 6, D_MODEL), lambda i: (mod_map1(i, tm), 0, 0)),
                  pl.BlockSpec((D_MODEL, 4 * NA_WIDTH), const, pipeline_mode=whole),
                  pl.BlockSpec((D_MODEL, LANES), const, pipeline_mode=whole)],
        out_specs=out_specs,
        compiler_params=_cparams(("parallel",)),
        name="input_projection",
    )(x2d, mod3, w["w_main"], w["w_rope"])


def _expand(lat, kr, w, *, rope, state, cos=None, sin=None):
    tm = EXPAND_TM
    tiles_per_batch = DEC_SEQ // tm
    const = lambda i: (0, 0)
    row = lambda i: (i, 0)
    wide = MLA_HEADS * MLA_QK_PAD
    in_specs = [pl.BlockSpec((tm, Q_LORA + KV_LORA), row), pl.BlockSpec((tm, LANES), row),
                pl.BlockSpec((1, Q_LORA), const), pl.BlockSpec((1, KV_LORA), const),
                pl.BlockSpec((Q_LORA, NA_WIDTH), const), pl.BlockSpec((Q_LORA, NA_WIDTH), const),
                pl.BlockSpec((KV_LORA, NA_WIDTH), const), pl.BlockSpec((KV_LORA, MLA_WIDTH), const)]
    args = [lat, kr, w["q_g"], w["kv_g"], w["w_qn"], w["w_qr"], w["w_kn"], w["w_v"]]
    if rope:
        in_specs += [pl.BlockSpec((tm, LANES), lambda i: (i % tiles_per_batch, 0))] * 2
        args += [cos, sin]
    out_shape = [jax.ShapeDtypeStruct((N_TOK, wide), BF), jax.ShapeDtypeStruct((N_TOK, wide), BF),
                 jax.ShapeDtypeStruct((N_TOK, MLA_WIDTH), BF)]
    out_specs = [pl.BlockSpec((tm, wide), row), pl.BlockSpec((tm, wide), row), pl.BlockSpec((tm, MLA_WIDTH), row)]
    if state:
        out_shape.append(jax.ShapeDtypeStruct((N_TOK, KV_LORA), F32))
        out_specs.append(pl.BlockSpec((tm, KV_LORA), row))
    return pl.pallas_call(
        functools.partial(_expand_kernel, rope=rope, state=state),
        out_shape=out_shape,
        grid=(N_TOK // tm,),
        in_specs=in_specs,
        out_specs=out_specs,
        compiler_params=_cparams(("parallel",)),
        name="lowrank_expand_latent" if rope else "lowrank_expand_context",
    )(*args)


def _cache_kv_kernel(ckv_ref, kr_ref, wkn_ref, wv_ref, km_ref, vm_ref):
    cb = ckv_ref[...].astype(BF)
    kn = _dot(cb, wkn_ref[...])
    vm_ref[...] = _dot(cb, wv_ref[...]).astype(BF)
    krb = kr_ref[...].astype(BF)
    for h in range(MLA_HEADS):
        km_ref[:, h * MLA_QK_PAD: h * MLA_QK_PAD + LANES] = kn[:, h * LANES:(h + 1) * LANES].astype(BF)
        km_ref[:, h * MLA_QK_PAD + LANES: (h + 1) * MLA_QK_PAD] = krb


def _cache_kv(ckv2d, kr2d, w):
    n = DEC_BATCH * PAST_LEN
    tm = PAST_LEN
    const = lambda i: (0, 0)
    row = lambda i: (i, 0)
    return pl.pallas_call(
        _cache_kv_kernel,
        out_shape=[jax.ShapeDtypeStruct((n, MLA_HEADS * MLA_QK_PAD), BF),
                   jax.ShapeDtypeStruct((n, MLA_WIDTH), BF)],
        grid=(n // tm,),
        in_specs=[pl.BlockSpec((tm, KV_LORA), row), pl.BlockSpec((tm, LANES), row),
                  pl.BlockSpec((KV_LORA, NA_WIDTH), const), pl.BlockSpec((KV_LORA, MLA_WIDTH), const)],
        out_specs=[pl.BlockSpec((tm, MLA_HEADS * MLA_QK_PAD), row), pl.BlockSpec((tm, MLA_WIDTH), row)],
        compiler_params=_cparams(("parallel",)),
        name="cache_kv_expand",
    )(ckv2d, kr2d, w["w_kn"], w["w_v"])


def _softmax_av(s_list, v_list):
    m = functools.reduce(jnp.maximum, [s.max(axis=-1, keepdims=True) for s in s_list])
    p_list = [jnp.exp2(s - m) for s in s_list]
    l = functools.reduce(jnp.add, [p.sum(axis=-1, keepdims=True) for p in p_list])
    o = functools.reduce(jnp.add, [_dot(p.astype(BF), v) for p, v in zip(p_list, v_list)])
    return o * (1.0 / l)


def _ctx_attn_kernel(q_ref, k_ref, v_ref, qm_ref, km_ref, vm_ref, ona_ref, omla_ref):
    for h in range(NA_HEADS):
        sl = slice(h * NA_HEAD_DIM, (h + 1) * NA_HEAD_DIM)
        s = _dot_nt(q_ref[:, sl], k_ref[:, sl])
        ona_ref[:, sl] = _softmax_av([s], [v_ref[:, sl]]).astype(BF)
    for h in range(MLA_HEADS):
        sq = slice(h * MLA_QK_PAD, (h + 1) * MLA_QK_PAD)
        sv = slice(h * MLA_V_DIM, (h + 1) * MLA_V_DIM)
        s = _dot_nt(qm_ref[:, sq], km_ref[:, sq])
        omla_ref[:, sv] = _softmax_av([s], [vm_ref[:, sv]]).astype(BF)


def _ctx_attention(naq, nak, nav, qm, km, vm):
    row = lambda b: (b, 0)
    wide = MLA_HEADS * MLA_QK_PAD
    return pl.pallas_call(
        _ctx_attn_kernel,
        out_shape=[jax.ShapeDtypeStruct((N_TOK, NA_WIDTH), BF), jax.ShapeDtypeStruct((N_TOK, MLA_WIDTH), BF)],
        grid=(BATCH,),
        in_specs=[pl.BlockSpec((SEQ, NA_WIDTH), row)] * 3
                 + [pl.BlockSpec((SEQ, wide), row)] * 2 + [pl.BlockSpec((SEQ, MLA_WIDTH), row)],
        out_specs=[pl.BlockSpec((SEQ, NA_WIDTH), row), pl.BlockSpec((SEQ, MLA_WIDTH), row)],
        compiler_params=_cparams(("parallel",)),
        name="context_attention",
    )(naq, nak, nav, qm, km, vm)


def _na_window_start(step):
    return jnp.clip(NA_QROWS * step - NA_WIN_ROWS // 2, 0, GRID_H - NA_KROWS)


def _lat_na_kernel(q_ref, k_ref, v_ref, kc_ref, vc_ref, bias_ref, o_ref):
    start = pl.multiple_of(_na_window_start(pl.program_id(1)) * GRID_W, NA_QTOK)
    for h in range(NA_HEADS):
        sl = slice(h * NA_HEAD_DIM, (h + 1) * NA_HEAD_DIM)
        q = q_ref[:, sl]
        s_loc = _dot_nt(q, k_ref[pl.ds(start, NA_WIN_TOK), sl]) + bias_ref[0, h]
        s_ctx = _dot_nt(q, kc_ref[:, sl])
        o = _softmax_av([s_loc, s_ctx], [v_ref[pl.ds(start, NA_WIN_TOK), sl], vc_ref[:, sl]])
        o_ref[:, sl] = o.astype(BF)


def _na_bias_variant(step):
    return jnp.where(step == 0, 0, jnp.where(step == NA_STEPS - 1, 2, 1))


def _lat_na_attention(naq, nak, nav, kc, vc, bias):
    whole = pl.Buffered(1)
    return pl.pallas_call(
        _lat_na_kernel,
        out_shape=jax.ShapeDtypeStruct((N_TOK, NA_WIDTH), BF),
        grid=(DEC_BATCH, NA_STEPS),
        in_specs=[pl.BlockSpec((NA_QTOK, NA_WIDTH), lambda b, j: (b * NA_STEPS + j, 0)),
                  pl.BlockSpec((DEC_SEQ, NA_WIDTH), lambda b, j: (b, 0), pipeline_mode=whole),
                  pl.BlockSpec((DEC_SEQ, NA_WIDTH), lambda b, j: (b, 0), pipeline_mode=whole),
                  pl.BlockSpec((PAST_LEN, NA_WIDTH), lambda b, j: (b, 0)),
                  pl.BlockSpec((PAST_LEN, NA_WIDTH), lambda b, j: (b, 0)),
                  pl.BlockSpec((1, NA_HEADS, NA_QTOK, NA_WIN_TOK), lambda b, j: (_na_bias_variant(j), 0, 0, 0))],
        out_specs=pl.BlockSpec((NA_QTOK, NA_WIDTH), lambda b, j: (b * NA_STEPS + j, 0)),
        compiler_params=_cparams(("parallel", "arbitrary")),
        name="latent_neighbourhood_attention",
    )(naq, nak, nav, kc, vc, bias)


def _lat_mla_kernel(q_ref, k_ref, v_ref, kc_ref, vc_ref, o_ref):
    heads = range(MLA_HPS)
    qk = [slice(h * MLA_QK_PAD, (h + 1) * MLA_QK_PAD) for h in heads]
    vv = [slice(h * MLA_V_DIM, (h + 1) * MLA_V_DIM) for h in heads]
    q = [q_ref[:, qk[h]] for h in heads]
    m, l, acc = [], [], []
    for h in heads:
        s = _dot_nt(q[h], kc_ref[:, qk[h]])
        m.append(s.max(axis=-1, keepdims=True))
        p = jnp.exp2(s - m[h])
        l.append(p.sum(axis=-1, keepdims=True))
        acc.append(_dot(p.astype(BF), vc_ref[:, vv[h]]))
    for c in range(DEC_SEQ // MLA_KB):
        ks = slice(c * MLA_KB, (c + 1) * MLA_KB)
        for h in heads:
            s = _dot_nt(q[h], k_ref[ks, qk[h]])
            m_new = jnp.maximum(m[h], s.max(axis=-1, keepdims=True))
            a = jnp.exp2(m[h] - m_new)
            p = jnp.exp2(s - m_new)
            l[h] = a * l[h] + p.sum(axis=-1, keepdims=True)
            acc[h] = a * acc[h] + _dot(p.astype(BF), v_ref[ks, vv[h]])
            m[h] = m_new
    for h in heads:
        o_ref[:, vv[h]] = (acc[h] * (1.0 / l[h])).astype(BF)


def _lat_mla_attention(qm, km, vm, kmc, vmc):
    nq = DEC_SEQ // MLA_TQ
    qw, vw = MLA_HPS * MLA_QK_PAD, MLA_HPS * MLA_V_DIM
    return pl.pallas_call(
        _lat_mla_kernel,
        out_shape=jax.ShapeDtypeStruct((N_TOK, MLA_WIDTH), BF),
        grid=(DEC_BATCH, MLA_HEADS // MLA_HPS, nq),
        in_specs=[pl.BlockSpec((MLA_TQ, qw), lambda b, h, i: (b * nq + i, h)),
                  pl.BlockSpec((DEC_SEQ, qw), lambda b, h, i: (b, h)),
                  pl.BlockSpec((DEC_SEQ, vw), lambda b, h, i: (b, h)),
                  pl.BlockSpec((PAST_LEN, qw), lambda b, h, i: (b, h)),
                  pl.BlockSpec((PAST_LEN, vw), lambda b, h, i: (b, h))],
        out_specs=pl.BlockSpec((MLA_TQ, vw), lambda b, h, i: (b * nq + i, h)),
        compiler_params=_cparams(("parallel", "parallel", "arbitrary")),
        name="latent_mla_attention",
    )(qm, km, vm, kmc, vmc)


def _oproj_kernel(ana_ref, amla_ref, x_ref, mod_ref, w1_ref, w2_ref, g_ref, b_ref, o_ref):
    o = _dot(ana_ref[...], w1_ref[...]) + _dot(amla_ref[...], w2_ref[...])
    y = ALPHA * x_ref[...] + mod_ref[0, 2:3, :] * o
    o_ref[...] = _layernorm(y, g_ref[...], b_ref[...])


def _out_projection(ana, amla, x2d, mod3, mod_map1, w_o, g, b):
    assert NA_WIDTH == MLA_WIDTH
    tm = OPROJ_TM
    row = lambda i: (i, 0)
    const = lambda i: (0, 0)
    whole = pl.Buffered(1)
    return pl.pallas_call(
        _oproj_kernel,
        out_shape=jax.ShapeDtypeStruct((N_TOK, D_MODEL), F32),
        grid=(N_TOK // tm,),
        in_specs=[pl.BlockSpec((tm, NA_WIDTH), row), pl.BlockSpec((tm, MLA_WIDTH), row),
                  pl.BlockSpec((tm, D_MODEL), row),
                  pl.BlockSpec((1, 6, D_MODEL), lambda i: (mod_map1(i, tm), 0, 0)),
                  pl.BlockSpec((NA_WIDTH, D_MODEL), const, pipeline_mode=whole),
                  pl.BlockSpec((MLA_WIDTH, D_MODEL), lambda i: (1, 0), pipeline_mode=whole),
                  pl.BlockSpec((1, D_MODEL), const), pl.BlockSpec((1, D_MODEL), const)],
        out_specs=pl.BlockSpec((tm, D_MODEL), row),
        compiler_params=_cparams(("parallel",)),
        name="out_projection_ln1",
    )(ana, amla, x2d, mod3, w_o, w_o, g, b)


def _ffn_kernel(x_ref, mod_ref, wg_ref, wu_ref, wd_ref, g_ref, b_ref, o_ref, h_scr):
    j = pl.program_id(1)

    @pl.when(j == 0)
    def _():
        h_scr[...] = (x_ref[...] * (1.0 + mod_ref[0, 4:5, :]) + mod_ref[0, 3:4, :]).astype(BF)
        o_ref[...] = jnp.zeros_like(o_ref)

    hb = h_scr[...]
    gate = _dot(hb, wg_ref[0])
    up = _dot(hb, wu_ref[0])
    act = (gate * _sigmoid(gate) * up).astype(BF)
    for n in range(D_MODEL // FFN_TN):
        sl = slice(n * FFN_TN, (n + 1) * FFN_TN)
        o_ref[:, sl] += _dot(act, wd_ref[:, sl])

    @pl.when(j == pl.num_programs(1) - 1)
    def _():
        y = ALPHA * x_ref[...] + mod_ref[0, 5:6, :] * o_ref[...]
        o_ref[...] = _layernorm(y, g_ref[...], b_ref[...])


def _ffn(x1, mod3, mod_map1, w_gu, w_down, g, b):
    tm, tf = FFN_TM, FFN_TF
    nf = D_FF // tf
    const = lambda i, j: (0, 0)
    row = lambda i, j: (i, 0)
    return pl.pallas_call(
        _ffn_kernel,
        out_shape=jax.ShapeDtypeStruct((N_TOK, D_MODEL), F32),
        grid=(N_TOK // tm, nf),
        in_specs=[pl.BlockSpec((tm, D_MODEL), row, pipeline_mode=pl.Buffered(1)),
                  pl.BlockSpec((1, 6, D_MODEL), lambda i, j: (mod_map1(i, tm), 0, 0)),
                  pl.BlockSpec((1, D_MODEL, tf), lambda i, j: (j, 0, 0)),
                  pl.BlockSpec((1, D_MODEL, tf), lambda i, j: (j + nf, 0, 0)),
                  pl.BlockSpec((tf, D_MODEL), lambda i, j: (j, 0)),
                  pl.BlockSpec((1, D_MODEL), const), pl.BlockSpec((1, D_MODEL), const)],
        out_specs=pl.BlockSpec((tm, D_MODEL), row),
        scratch_shapes=[pltpu.VMEM((tm, D_MODEL), BF)],
        compiler_params=_cparams(("parallel", "arbitrary")),
        name="swiglu_ln2",
    )(x1, mod3, w_gu, w_gu, w_down, g, b)


def _prep_weights(w_in, q_a_norm, kv_a_norm, w_q_b, w_kv_b):
    w_in_b = w_in.astype(BF)
    wq = w_q_b.astype(BF).reshape(Q_LORA, MLA_HEADS, MLA_NOPE_DIM + MLA_ROPE_DIM)
    wkv = w_kv_b.astype(BF).reshape(KV_LORA, MLA_HEADS, MLA_NOPE_DIM + MLA_V_DIM)
    pad = LANES - MLA_ROPE_DIM
    return {
        "w_main": w_in_b,
        "w_rope": jnp.pad(w_in_b[:, 4096:], ((0, 0), (0, pad))),
        "q_g": q_a_norm.reshape(1, Q_LORA),
        "kv_g": kv_a_norm.reshape(1, KV_LORA),
        "w_qn": wq[:, :, :MLA_NOPE_DIM].reshape(Q_LORA, NA_WIDTH),
        "w_qr": jnp.pad(wq[:, :, MLA_NOPE_DIM:], ((0, 0), (0, 0), (0, pad))).reshape(Q_LORA, NA_WIDTH),
        "w_kn": wkv[:, :, :MLA_NOPE_DIM].reshape(KV_LORA, NA_WIDTH),
        "w_v": wkv[:, :, MLA_NOPE_DIM:].reshape(KV_LORA, MLA_WIDTH),
    }


def _rope_tables():
    half = MLA_ROPE_DIM // 2
    inv_freq = np.float32(ROPE_THETA) ** (-np.arange(0, half, 2, dtype=np.float32) / np.float32(half))
    t = np.arange(DEC_SEQ, dtype=np.int32)

    def tables(pos):
        ang = pos.astype(np.float32)[:, None] * inv_freq
        s = np.sin(ang)
        return np.concatenate([np.cos(ang)] * 2, -1), np.concatenate([-s, s], -1)

    cr, sr = tables(t // GRID_W)
    cc, sc = tables(t % GRID_W)
    pad = LANES - MLA_ROPE_DIM
    cos = np.concatenate([cr, cc, np.ones((DEC_SEQ, pad), np.float32)], -1)
    sin = np.concatenate([sr, sc, np.zeros((DEC_SEQ, pad), np.float32)], -1)
    return jnp.asarray(cos, F32), jnp.asarray(sin, F32)


def _na_bias_plan():
    plan = []
    for step in NA_VARIANT_STEPS:
        start = min(max(NA_QROWS * step - NA_WIN_ROWS // 2, 0), GRID_H - NA_KROWS)
        per_row = []
        for i in range(NA_QROWS):
            r = NA_QROWS * step + i
            rs = min(max(r - NA_WIN_ROWS // 2, 0), GRID_H - NA_WIN_ROWS)
            per_row.append([(start + u - r + NA_WIN_ROWS - 1, rs <= start + u < rs + NA_WIN_ROWS)
                            for u in range(NA_KROWS)])
        plan.append(per_row)
    return plan


def _na_bias_kernel(ra_ref, rb_ref, o_ref):
    shape = (GRID_W, LANES)
    lane = lax.broadcasted_iota(jnp.int32, shape, 1)
    qc = lax.broadcasted_iota(jnp.int32, shape, 0)
    kc = lane % GRID_W
    cs = jnp.clip(qc - NA_WIN_COLS // 2, 0, GRID_W - NA_WIN_COLS)
    ok_col = (kc >= cs) & (kc < cs + NA_WIN_COLS)
    left = lane < GRID_W
    neg = jnp.full(shape, NEG, F32)
    for v, per_row in enumerate(_na_bias_plan()):
        for i, rows in enumerate(per_row):
            for t in range(len(rows) // 2):
                (dra, oka), (drb, okb) = rows[2 * t], rows[2 * t + 1]
                if oka and okb:
                    ok = ok_col
                elif oka:
                    ok = ok_col & left
                elif okb:
                    ok = ok_col & jnp.logical_not(left)
                for h in range(NA_HEADS):
                    if oka or okb:
                        src = jnp.zeros((1, LANES), F32)
                        if oka:
                            src = src + ra_ref[h, dra:dra + 1, :]
                        if okb:
                            src = src + rb_ref[h, drb:drb + 1, :]
                        val = pltpu.roll(jnp.broadcast_to(src, shape), LANES - (NA_WIN_COLS - 1), 1,
                                         stride=1, stride_axis=0)
                        tile = jnp.where(ok, val * LOG2E, neg)
                    else:
                        tile = neg
                    o_ref[v, h, i * GRID_W:(i + 1) * GRID_W, t * LANES:(t + 1) * LANES] = tile


def _na_bias_tables(rpb):
    n_dr, n_dc = 2 * NA_WIN_ROWS - 1, 2 * NA_WIN_COLS - 1
    ra = jnp.pad(rpb, ((0, 0), (0, 16 - n_dr), (0, LANES - n_dc)))
    rb = jnp.pad(rpb, ((0, 0), (0, 16 - n_dr), (GRID_W, LANES - GRID_W - n_dc)))
    return pl.pallas_call(
        _na_bias_kernel,
        out_shape=jax.ShapeDtypeStruct((len(NA_VARIANT_STEPS), NA_HEADS, NA_QTOK, NA_WIN_TOK), F32),
        compiler_params=pltpu.CompilerParams(vmem_limit_bytes=VMEM_LIMIT),
        name="na_bias_tables",
    )(ra, rb)


def kernel(x_prompt, x_sample, cache_na_k, cache_na_v, cache_mla_ckv, cache_mla_krope, c, c_ctx,
           w_mod, b_mod, w_in, q_a_norm, kv_a_norm, w_q_b, w_kv_b, na_rpb, w_o,
           ln1_g, ln1_b, w_gu, w_down, ln2_g, ln2_b):
    cond8 = jnp.concatenate([c_ctx[None], c, jnp.zeros((8 - 1 - DEC_BATCH, D_MODEL), F32)], 0)
    mod3 = _modulation(cond8, w_mod[0], b_mod[0][None]).reshape(8, 6, D_MODEL)

    w = _prep_weights(w_in[0], q_a_norm[0], kv_a_norm[0], w_q_b[0], w_kv_b[0])
    w_o_b = w_o[0].astype(BF)
    w_gu_b = w_gu[0].astype(BF).reshape(D_MODEL, 2 * D_FF // FFN_TF, FFN_TF).transpose(1, 0, 2)
    w_down_b = w_down[0].astype(BF)
    g1, b1 = ln1_g[0][None], ln1_b[0][None]
    g2, b2 = ln2_g[0][None], ln2_b[0][None]
    cos, sin = _rope_tables()

    xp = x_prompt.reshape(N_TOK, D_MODEL)
    xs = x_sample.reshape(N_TOK, D_MODEL)
    ctx_mod = lambda i, tm: 0
    lat_mod = lambda i, tm: 1 + i // (DEC_SEQ // tm)

    naq, nak, nav, lat, kr, st_k, st_v = _projection(xp, mod3, ctx_mod, w, state=True)
    qm, km, vm, st_ckv = _expand(lat, kr, w, rope=False, state=True)
    st_kr = kr[:, :MLA_ROPE_DIM]
    a_na, a_mla = _ctx_attention(naq, nak, nav, qm, km, vm)
    x1 = _out_projection(a_na, a_mla, xp, mod3, ctx_mod, w_o_b, g1, b1)
    yp = _ffn(x1, mod3, ctx_mod, w_gu_b, w_down_b, g2, b2)

    naq, nak, nav, lat, kr = _projection(xs, mod3, lat_mod, w, state=False)
    qm, km, vm = _expand(lat, kr, w, rope=True, state=False, cos=cos, sin=sin)
    kr_cache = jnp.pad(cache_mla_krope[:, 0].reshape(DEC_BATCH * PAST_LEN, MLA_ROPE_DIM),
                       ((0, 0), (0, LANES - MLA_ROPE_DIM)))
    kmc, vmc = _cache_kv(cache_mla_ckv[:, 0].reshape(DEC_BATCH * PAST_LEN, KV_LORA), kr_cache, w)
    kc = cache_na_k[:, 0].reshape(DEC_BATCH * PAST_LEN, NA_WIDTH).astype(BF)
    vc = cache_na_v[:, 0].reshape(DEC_BATCH * PAST_LEN, NA_WIDTH).astype(BF)
    a_na = _lat_na_attention(naq, nak, nav, kc, vc, _na_bias_tables(na_rpb[0]))
    a_mla = _lat_mla_attention(qm, km, vm, kmc, vmc)
    x1 = _out_projection(a_na, a_mla, xs, mod3, lat_mod, w_o_b, g1, b1)
    ys = _ffn(x1, mod3, lat_mod, w_gu_b, w_down_b, g2, b2)

    return (yp.reshape(BATCH, SEQ, D_MODEL),
            ys.reshape(DEC_BATCH, DEC_SEQ, D_MODEL),
            st_k.reshape(BATCH, 1, SEQ, NA_HEADS, NA_HEAD_DIM),
            st_v.reshape(BATCH, 1, SEQ, NA_HEADS, NA_HEAD_DIM),
            st_ckv.reshape(BATCH, 1, SEQ, KV_LORA),
            st_kr.reshape(BATCH, 1, SEQ, MLA_ROPE_DIM))
```

```python
import functools

import numpy as np
import jax
import jax.numpy as jnp
from jax import lax
from jax.experimental import pallas as pl
from jax.experimental.pallas import tpu as pltpu

D_MODEL = 2048
BATCH = 32
SEQ = 256
DEC_BATCH = 2
DEC_SEQ = 4096
PAST_LEN = 512
GRID_W = 64
GRID_H = DEC_SEQ // GRID_W
NA_HEADS = 8
NA_HEAD_DIM = 128
NA_WIN_ROWS = 8
NA_WIN_COLS = 16
MLA_HEADS = 8
MLA_NOPE_DIM = 128
MLA_ROPE_DIM = 64
MLA_V_DIM = 128
Q_LORA = 512
KV_LORA = 512
NA_WIDTH = NA_HEADS * NA_HEAD_DIM
MLA_WIDTH = MLA_HEADS * MLA_V_DIM
D_FF = -(-8 * D_MODEL // (3 * 256)) * 256
ROPE_THETA = 10000.0
LN_EPS = 1e-5
RMS_EPS = 1e-6
ALPHA = 2.0 ** 0.25
LOG2E = 1.4426950408889634
NA_SCALE = NA_HEAD_DIM ** -0.5 * LOG2E
MLA_SCALE = (MLA_NOPE_DIM + MLA_ROPE_DIM) ** -0.5 * LOG2E

BF = jnp.bfloat16
F32 = jnp.float32
LANES = 128
MLA_QK_PAD = 2 * LANES
NEG = -1e30
VMEM_LIMIT = 60 * 1024 * 1024

N_TOK = BATCH * SEQ
assert N_TOK == DEC_BATCH * DEC_SEQ

PROJ_TM = 512
EXPAND_TM = 1024
CTX_BPS = 2
OPROJ_TM = 512
FFN_TM = 1024
FFN_TF = 512
FFN_TN = 512
MLA_TQ = 1024
MLA_KB = 512
MLA_HPS = 2
NA_QROWS = 4
NA_STEPS = GRID_H // NA_QROWS
NA_KROWS = NA_QROWS + NA_WIN_ROWS
NA_QTOK = NA_QROWS * GRID_W
NA_WIN_TOK = NA_KROWS * GRID_W
NA_VARIANT_STEPS = (0, 1, NA_STEPS - 1)
assert NA_QROWS == NA_WIN_ROWS // 2 and NA_KROWS % 2 == 0


def _cparams(sem):
    return pltpu.CompilerParams(dimension_semantics=sem, vmem_limit_bytes=VMEM_LIMIT)


def _dot(a, b):
    return jnp.dot(a, b, preferred_element_type=F32)


def _dot_nt(a, b):
    return lax.dot_general(a, b, (((1,), (1,)), ((), ())), preferred_element_type=F32)


def _sigmoid(x):
    return 1.0 / (1.0 + jnp.exp(-x))


def _layernorm(y, g, b):
    mu = jnp.mean(y, axis=-1, keepdims=True)
    yc = y - mu
    var = jnp.mean(yc * yc, axis=-1, keepdims=True)
    return yc * lax.rsqrt(var + LN_EPS) * g + b


def _rmsnorm(x, g):
    return x * lax.rsqrt(jnp.mean(x * x, axis=-1, keepdims=True) + RMS_EPS) * g


def _mod_kernel(c_ref, w_ref, b_ref, o_ref):
    c = c_ref[...]
    s = (c * _sigmoid(c)).astype(BF)
    o_ref[...] = _dot(s, w_ref[...].astype(BF)) + b_ref[...]


def _modulation(cond8, w_mod, b_mod):
    tn = 1024
    n = 6 * D_MODEL
    return pl.pallas_call(
        _mod_kernel,
        out_shape=jax.ShapeDtypeStruct((8, n), F32),
        grid=(n // tn,),
        in_specs=[pl.BlockSpec((8, D_MODEL), lambda j: (0, 0)),
                  pl.BlockSpec((D_MODEL, tn), lambda j: (0, j)),
                  pl.BlockSpec((1, tn), lambda j: (0, j))],
        out_specs=pl.BlockSpec((8, tn), lambda j: (0, j)),
        compiler_params=_cparams(("parallel",)),
        name="modulation",
    )(cond8, w_mod, b_mod)


def _rope128(x, cos, sin):
    lane = lax.broadcasted_iota(jnp.int32, x.shape, 1)
    partner = jnp.where(lane % 32 < 16, pltpu.roll(x, LANES - 16, 1), pltpu.roll(x, 16, 1))
    return x * cos + partner * sin


def _proj_kernel(x_ref, mod_ref, wm_ref, wr_ref, naq_ref, nak_ref, nav_ref, lat_ref, kr_ref, *state_refs):
    hb = (x_ref[...] * (1.0 + mod_ref[0, 1:2, :]) + mod_ref[0, 0:1, :]).astype(BF)
    outs = (naq_ref, nak_ref, nav_ref, lat_ref)
    for n, o_ref in enumerate(outs):
        z = _dot(hb, wm_ref[:, n * NA_WIDTH:(n + 1) * NA_WIDTH])
        if state_refs and n in (1, 2):
            state_refs[n - 1][...] = z.reshape(z.shape[0], NA_HEADS, NA_HEAD_DIM)
        if n == 0:
            z = z * NA_SCALE
        o_ref[...] = z.astype(o_ref.dtype)
    kr_ref[...] = _dot(hb, wr_ref[...])


def _expand_kernel(*refs, rope, state):
    it = iter(refs)
    lat_ref, kr_ref, qg_ref, kvg_ref, wqn_ref, wqr_ref, wkn_ref, wv_ref = (next(it) for _ in range(8))
    cos_ref, sin_ref = (next(it), next(it)) if rope else (None, None)
    qm_ref, km_ref, vm_ref = (next(it) for _ in range(3))
    ckv_ref = next(it) if state else None

    qn = _rmsnorm(lat_ref[:, :Q_LORA], qg_ref[...]).astype(BF)
    qnope = _dot(qn, wqn_ref[...]) * MLA_SCALE
    qrope = _dot(qn, wqr_ref[...]) * MLA_SCALE
    for h in range(MLA_HEADS):
        sl = slice(h * LANES, (h + 1) * LANES)
        r = qrope[:, sl]
        if rope:
            r = _rope128(r, cos_ref[...], sin_ref[...])
        qm_ref[:, h * MLA_QK_PAD: h * MLA_QK_PAD + LANES] = qnope[:, sl].astype(BF)
        qm_ref[:, h * MLA_QK_PAD + LANES: (h + 1) * MLA_QK_PAD] = r.astype(BF)

    ckv = _rmsnorm(lat_ref[:, Q_LORA:], kvg_ref[...])
    if state:
        ckv_ref[...] = ckv
    cb = ckv.astype(BF)
    kn = _dot(cb, wkn_ref[...])
    vm_ref[...] = _dot(cb, wv_ref[...]).astype(BF)
    kr = kr_ref[...]
    if rope:
        kr = _rope128(kr, cos_ref[...], sin_ref[...])
    krb = kr.astype(BF)
    for h in range(MLA_HEADS):
        km_ref[:, h * MLA_QK_PAD: h * MLA_QK_PAD + LANES] = kn[:, h * LANES:(h + 1) * LANES].astype(BF)
        km_ref[:, h * MLA_QK_PAD + LANES: (h + 1) * MLA_QK_PAD] = krb


def _projection(x2d, mod3, mod_map1, w, *, state):
    tm = PROJ_TM
    const = lambda i: (0, 0)
    row = lambda i: (i, 0)
    whole = pl.Buffered(1)
    out_shape = [jax.ShapeDtypeStruct((N_TOK, NA_WIDTH), BF)] * 3 + [
        jax.ShapeDtypeStruct((N_TOK, Q_LORA + KV_LORA), F32), jax.ShapeDtypeStruct((N_TOK, LANES), F32)]
    out_specs = [pl.BlockSpec((tm, NA_WIDTH), row)] * 3 + [
        pl.BlockSpec((tm, Q_LORA + KV_LORA), row), pl.BlockSpec((tm, LANES), row)]
    if state:
        out_shape += [jax.ShapeDtypeStruct((N_TOK, NA_HEADS, NA_HEAD_DIM), F32)] * 2
        out_specs += [pl.BlockSpec((tm, NA_HEADS, NA_HEAD_DIM), lambda i: (i, 0, 0))] * 2
    return pl.pallas_call(
        _proj_kernel,
        out_shape=out_shape,
        grid=(N_TOK // tm,),
        in_specs=[pl.BlockSpec((tm, D_MODEL), row),
                  pl.BlockSpec((1, 6, D_MODEL), lambda i: (mod_map1(i, tm), 0, 0)),
                  pl.BlockSpec((D_MODEL, 4 * NA_WIDTH), const, pipeline_mode=whole),
                  pl.BlockSpec((D_MODEL, LANES), const, pipeline_mode=whole)],
        out_specs=out_specs,
        compiler_params=_cparams(("parallel",)),
        name="input_projection",
    )(x2d, mod3, w["w_main"], w["w_rope"])


def _expand(lat, kr, w, *, rope, state, cos=None, sin=None):
    tm = EXPAND_TM
    tiles_per_batch = DEC_SEQ // tm
    const = lambda i: (0, 0)
    row = lambda i: (i, 0)
    wide = MLA_HEADS * MLA_QK_PAD
    in_specs = [pl.BlockSpec((tm, Q_LORA + KV_LORA), row), pl.BlockSpec((tm, LANES), row),
                pl.BlockSpec((1, Q_LORA), const), pl.BlockSpec((1, KV_LORA), const),
                pl.BlockSpec((Q_LORA, NA_WIDTH), const), pl.BlockSpec((Q_LORA, NA_WIDTH), const),
                pl.BlockSpec((KV_LORA, NA_WIDTH), const), pl.BlockSpec((KV_LORA, MLA_WIDTH), const)]
    args = [lat, kr, w["q_g"], w["kv_g"], w["w_qn"], w["w_qr"], w["w_kn"], w["w_v"]]
    if rope:
        in_specs += [pl.BlockSpec((tm, LANES), lambda i: (i % tiles_per_batch, 0))] * 2
        args += [cos, sin]
    out_shape = [jax.ShapeDtypeStruct((N_TOK, wide), BF), jax.ShapeDtypeStruct((N_TOK, wide), BF),
                 jax.ShapeDtypeStruct((N_TOK, MLA_WIDTH), BF)]
    out_specs = [pl.BlockSpec((tm, wide), row), pl.BlockSpec((tm, wide), row), pl.BlockSpec((tm, MLA_WIDTH), row)]
    if state:
        out_shape.append(jax.ShapeDtypeStruct((N_TOK, KV_LORA), F32))
        out_specs.append(pl.BlockSpec((tm, KV_LORA), row))
    return pl.pallas_call(
        functools.partial(_expand_kernel, rope=rope, state=state),
        out_shape=out_shape,
        grid=(N_TOK // tm,),
        in_specs=in_specs,
        out_specs=out_specs,
        compiler_params=_cparams(("parallel",)),
        name="lowrank_expand_latent" if rope else "lowrank_expand_context",
    )(*args)


def _cache_kv_kernel(ckv_ref, kr_ref, wkn_ref, wv_ref, km_ref, vm_ref):
    cb = ckv_ref[...].astype(BF)
    kn = _dot(cb, wkn_ref[...])
    vm_ref[...] = _dot(cb, wv_ref[...]).astype(BF)
    krb = kr_ref[...].astype(BF)
    for h in range(MLA_HEADS):
        km_ref[:, h * MLA_QK_PAD: h * MLA_QK_PAD + LANES] = kn[:, h * LANES:(h + 1) * LANES].astype(BF)
        km_ref[:, h * MLA_QK_PAD + LANES: (h + 1) * MLA_QK_PAD] = krb


def _cache_kv(ckv2d, kr2d, w):
    n = DEC_BATCH * PAST_LEN
    tm = PAST_LEN
    const = lambda i: (0, 0)
    row = lambda i: (i, 0)
    return pl.pallas_call(
        _cache_kv_kernel,
        out_shape=[jax.ShapeDtypeStruct((n, MLA_HEADS * MLA_QK_PAD), BF),
                   jax.ShapeDtypeStruct((n, MLA_WIDTH), BF)],
        grid=(n // tm,),
        in_specs=[pl.BlockSpec((tm, KV_LORA), row), pl.BlockSpec((tm, LANES), row),
                  pl.BlockSpec((KV_LORA, NA_WIDTH), const), pl.BlockSpec((KV_LORA, MLA_WIDTH), const)],
        out_specs=[pl.BlockSpec((tm, MLA_HEADS * MLA_QK_PAD), row), pl.BlockSpec((tm, MLA_WIDTH), row)],
        compiler_params=_cparams(("parallel",)),
        name="cache_kv_expand",
    )(ckv2d, kr2d, w["w_kn"], w["w_v"])


def _softmax_av(s_list, v_list):
    m = functools.reduce(jnp.maximum, [s.max(axis=-1, keepdims=True) for s in s_list])
    p_list = [jnp.exp2(s - m) for s in s_list]
    l = functools.reduce(jnp.add, [p.sum(axis=-1, keepdims=True) for p in p_list])
    o = functools.reduce(jnp.add, [_dot(p.astype(BF), v) for p, v in zip(p_list, v_list)])
    return o * (1.0 / l)


def _cast_rows(src_refs, dst_refs):
    for src, dst in zip(src_refs, dst_refs):
        dst[...] = src[...].astype(BF)


def _cast_specs(weights, steps):
    in_specs, out_specs, out_shape = [], [], []
    for wgt in weights:
        rows, cols = wgt.shape
        blk = pl.BlockSpec((rows // steps, cols), lambda i: (i, 0))
        in_specs.append(blk)
        out_specs.append(blk)
        out_shape.append(jax.ShapeDtypeStruct((rows, cols), BF))
    return in_specs, out_specs, out_shape


def _ctx_attn_kernel(q_ref, k_ref, v_ref, qm_ref, km_ref, vm_ref, *rest):
    n_cast = (len(rest) - 2) // 2
    ona_ref, omla_ref = rest[n_cast:n_cast + 2]
    _cast_rows(rest[:n_cast], rest[n_cast + 2:])
    for b in range(CTX_BPS):
        rows = slice(b * SEQ, (b + 1) * SEQ)
        for h in range(NA_HEADS):
            sl = slice(h * NA_HEAD_DIM, (h + 1) * NA_HEAD_DIM)
            s = _dot_nt(q_ref[rows, sl], k_ref[rows, sl])
            ona_ref[rows, sl] = _softmax_av([s], [v_ref[rows, sl]]).astype(BF)
        for h in range(MLA_HEADS):
            sq = slice(h * MLA_QK_PAD, (h + 1) * MLA_QK_PAD)
            sv = slice(h * MLA_V_DIM, (h + 1) * MLA_V_DIM)
            s = _dot_nt(qm_ref[rows, sq], km_ref[rows, sq])
            omla_ref[rows, sv] = _softmax_av([s], [vm_ref[rows, sv]]).astype(BF)


def _ctx_attention(naq, nak, nav, qm, km, vm, cast_weights):
    row = lambda b: (b, 0)
    wide = MLA_HEADS * MLA_QK_PAD
    tb = CTX_BPS * SEQ
    steps = BATCH // CTX_BPS
    c_in, c_out, c_shape = _cast_specs(cast_weights, steps)
    return pl.pallas_call(
        _ctx_attn_kernel,
        out_shape=[jax.ShapeDtypeStruct((N_TOK, NA_WIDTH), BF), jax.ShapeDtypeStruct((N_TOK, MLA_WIDTH), BF)]
                  + c_shape,
        grid=(steps,),
        in_specs=[pl.BlockSpec((tb, NA_WIDTH), row)] * 3
                 + [pl.BlockSpec((tb, wide), row)] * 2 + [pl.BlockSpec((tb, MLA_WIDTH), row)] + c_in,
        out_specs=[pl.BlockSpec((tb, NA_WIDTH), row), pl.BlockSpec((tb, MLA_WIDTH), row)] + c_out,
        compiler_params=_cparams(("parallel",)),
        name="context_attention",
    )(naq, nak, nav, qm, km, vm, *cast_weights)


def _na_window_start(step):
    return jnp.clip(NA_QROWS * step - NA_WIN_ROWS // 2, 0, GRID_H - NA_KROWS)


def _lat_na_kernel(q_ref, k_ref, v_ref, kc_ref, vc_ref, bias_ref, o_ref):
    start = pl.multiple_of(_na_window_start(pl.program_id(1)) * GRID_W, NA_QTOK)
    for h in range(NA_HEADS):
        sl = slice(h * NA_HEAD_DIM, (h + 1) * NA_HEAD_DIM)
        q = q_ref[:, sl]
        s_loc = _dot_nt(q, k_ref[pl.ds(start, NA_WIN_TOK), sl]) + bias_ref[0, h]
        s_ctx = _dot_nt(q, kc_ref[:, sl])
        o = _softmax_av([s_loc, s_ctx], [v_ref[pl.ds(start, NA_WIN_TOK), sl], vc_ref[:, sl]])
        o_ref[:, sl] = o.astype(BF)


def _na_bias_variant(step):
    return jnp.where(step == 0, 0, jnp.where(step == NA_STEPS - 1, 2, 1))


def _lat_na_attention(naq, nak, nav, kc, vc, bias):
    whole = pl.Buffered(1)
    return pl.pallas_call(
        _lat_na_kernel,
        out_shape=jax.ShapeDtypeStruct((N_TOK, NA_WIDTH), BF),
        grid=(DEC_BATCH, NA_STEPS),
        in_specs=[pl.BlockSpec((NA_QTOK, NA_WIDTH), lambda b, j: (b * NA_STEPS + j, 0)),
                  pl.BlockSpec((DEC_SEQ, NA_WIDTH), lambda b, j: (b, 0), pipeline_mode=whole),
                  pl.BlockSpec((DEC_SEQ, NA_WIDTH), lambda b, j: (b, 0), pipeline_mode=whole),
                  pl.BlockSpec((PAST_LEN, NA_WIDTH), lambda b, j: (b, 0)),
                  pl.BlockSpec((PAST_LEN, NA_WIDTH), lambda b, j: (b, 0)),
                  pl.BlockSpec((1, NA_HEADS, NA_QTOK, NA_WIN_TOK), lambda b, j: (_na_bias_variant(j), 0, 0, 0))],
        out_specs=pl.BlockSpec((NA_QTOK, NA_WIDTH), lambda b, j: (b * NA_STEPS + j, 0)),
        compiler_params=_cparams(("parallel", "arbitrary")),
        name="latent_neighbourhood_attention",
    )(naq, nak, nav, kc, vc, bias)


def _lat_mla_kernel(q_ref, k_ref, v_ref, kc_ref, vc_ref, o_ref):
    heads = range(MLA_HPS)
    qk = [slice(h * MLA_QK_PAD, (h + 1) * MLA_QK_PAD) for h in heads]
    vv = [slice(h * MLA_V_DIM, (h + 1) * MLA_V_DIM) for h in heads]
    q = [q_ref[:, qk[h]] for h in heads]
    m, l, acc = [], [], []
    for h in heads:
        s = _dot_nt(q[h], kc_ref[:, qk[h]])
        m.append(s.max(axis=-1, keepdims=True))
        p = jnp.exp2(s - m[h])
        l.append(p.sum(axis=-1, keepdims=True))
        acc.append(_dot(p.astype(BF), vc_ref[:, vv[h]]))
    for c in range(DEC_SEQ // MLA_KB):
        ks = slice(c * MLA_KB, (c + 1) * MLA_KB)
        for h in heads:
            s = _dot_nt(q[h], k_ref[ks, qk[h]])
            m_new = jnp.maximum(m[h], s.max(axis=-1, keepdims=True))
            a = jnp.exp2(m[h] - m_new)
            p = jnp.exp2(s - m_new)
            l[h] = a * l[h] + p.sum(axis=-1, keepdims=True)
            acc[h] = a * acc[h] + _dot(p.astype(BF), v_ref[ks, vv[h]])
            m[h] = m_new
    for h in heads:
        o_ref[:, vv[h]] = (acc[h] * (1.0 / l[h])).astype(BF)


def _lat_mla_attention(qm, km, vm, kmc, vmc):
    nq = DEC_SEQ // MLA_TQ
    qw, vw = MLA_HPS * MLA_QK_PAD, MLA_HPS * MLA_V_DIM
    return pl.pallas_call(
        _lat_mla_kernel,
        out_shape=jax.ShapeDtypeStruct((N_TOK, MLA_WIDTH), BF),
        grid=(DEC_BATCH, MLA_HEADS // MLA_HPS, nq),
        in_specs=[pl.BlockSpec((MLA_TQ, qw), lambda b, h, i: (b * nq + i, h)),
                  pl.BlockSpec((DEC_SEQ, qw), lambda b, h, i: (b, h)),
                  pl.BlockSpec((DEC_SEQ, vw), lambda b, h, i: (b, h)),
                  pl.BlockSpec((PAST_LEN, qw), lambda b, h, i: (b, h)),
                  pl.BlockSpec((PAST_LEN, vw), lambda b, h, i: (b, h))],
        out_specs=pl.BlockSpec((MLA_TQ, vw), lambda b, h, i: (b * nq + i, h)),
        compiler_params=_cparams(("parallel", "parallel", "arbitrary")),
        name="latent_mla_attention",
    )(qm, km, vm, kmc, vmc)


def _oproj_kernel(ana_ref, amla_ref, x_ref, mod_ref, w1_ref, w2_ref, g_ref, b_ref, *rest):
    n_cast = (len(rest) - 1) // 2
    o_ref = rest[n_cast]
    _cast_rows(rest[:n_cast], rest[n_cast + 1:])
    o = _dot(ana_ref[...], w1_ref[...]) + _dot(amla_ref[...], w2_ref[...])
    y = ALPHA * x_ref[...] + mod_ref[0, 2:3, :] * o
    o_ref[...] = _layernorm(y, g_ref[...], b_ref[...])


def _out_projection(ana, amla, x2d, mod3, mod_map1, w_o, g, b, cast_weights=()):
    assert NA_WIDTH == MLA_WIDTH
    tm = OPROJ_TM
    steps = N_TOK // tm
    row = lambda i: (i, 0)
    const = lambda i: (0, 0)
    whole = pl.Buffered(1)
    c_in, c_out, c_shape = _cast_specs(cast_weights, steps)
    return pl.pallas_call(
        _oproj_kernel,
        out_shape=[jax.ShapeDtypeStruct((N_TOK, D_MODEL), F32)] + c_shape,
        grid=(steps,),
        in_specs=[pl.BlockSpec((tm, NA_WIDTH), row), pl.BlockSpec((tm, MLA_WIDTH), row),
                  pl.BlockSpec((tm, D_MODEL), row),
                  pl.BlockSpec((1, 6, D_MODEL), lambda i: (mod_map1(i, tm), 0, 0)),
                  pl.BlockSpec((NA_WIDTH, D_MODEL), const, pipeline_mode=whole),
                  pl.BlockSpec((MLA_WIDTH, D_MODEL), lambda i: (1, 0), pipeline_mode=whole),
                  pl.BlockSpec((1, D_MODEL), const), pl.BlockSpec((1, D_MODEL), const)] + c_in,
        out_specs=[pl.BlockSpec((tm, D_MODEL), row)] + c_out,
        compiler_params=_cparams(("parallel",)),
        name="out_projection_ln1",
    )(ana, amla, x2d, mod3, w_o, w_o, g, b, *cast_weights)


def _ffn_kernel(x_ref, mod_ref, wg_ref, wu_ref, wd_ref, g_ref, b_ref, o_ref, h_scr):
    j = pl.program_id(1)

    @pl.when(j == 0)
    def _():
        h_scr[...] = (x_ref[...] * (1.0 + mod_ref[0, 4:5, :]) + mod_ref[0, 3:4, :]).astype(BF)
        o_ref[...] = jnp.zeros_like(o_ref)

    hb = h_scr[...]
    gate = _dot(hb, wg_ref[...])
    up = _dot(hb, wu_ref[...])
    act = (gate * _sigmoid(gate) * up).astype(BF)
    for n in range(D_MODEL // FFN_TN):
        sl = slice(n * FFN_TN, (n + 1) * FFN_TN)
        o_ref[:, sl] += _dot(act, wd_ref[:, sl])

    @pl.when(j == pl.num_programs(1) - 1)
    def _():
        y = ALPHA * x_ref[...] + mod_ref[0, 5:6, :] * o_ref[...]
        o_ref[...] = _layernorm(y, g_ref[...], b_ref[...])


def _ffn(x1, mod3, mod_map1, w_gu, w_down, g, b):
    tm, tf = FFN_TM, FFN_TF
    nf = D_FF // tf
    const = lambda i, j: (0, 0)
    row = lambda i, j: (i, 0)
    return pl.pallas_call(
        _ffn_kernel,
        out_shape=jax.ShapeDtypeStruct((N_TOK, D_MODEL), F32),
        grid=(N_TOK // tm, nf),
        in_specs=[pl.BlockSpec((tm, D_MODEL), row, pipeline_mode=pl.Buffered(1)),
                  pl.BlockSpec((1, 6, D_MODEL), lambda i, j: (mod_map1(i, tm), 0, 0)),
                  pl.BlockSpec((D_MODEL, tf), lambda i, j: (0, j)),
                  pl.BlockSpec((D_MODEL, tf), lambda i, j: (0, j + nf)),
                  pl.BlockSpec((tf, D_MODEL), lambda i, j: (j, 0)),
                  pl.BlockSpec((1, D_MODEL), const), pl.BlockSpec((1, D_MODEL), const)],
        out_specs=pl.BlockSpec((tm, D_MODEL), row),
        scratch_shapes=[pltpu.VMEM((tm, D_MODEL), BF)],
        compiler_params=_cparams(("parallel", "arbitrary")),
        name="swiglu_ln2",
    )(x1, mod3, w_gu, w_gu, w_down, g, b)


def _prep_weights(w_in, q_a_norm, kv_a_norm, w_q_b, w_kv_b):
    w_in_b = w_in.astype(BF)
    wq = w_q_b.astype(BF).reshape(Q_LORA, MLA_HEADS, MLA_NOPE_DIM + MLA_ROPE_DIM)
    wkv = w_kv_b.astype(BF).reshape(KV_LORA, MLA_HEADS, MLA_NOPE_DIM + MLA_V_DIM)
    pad = LANES - MLA_ROPE_DIM
    return {
        "w_main": w_in_b,
        "w_rope": jnp.pad(w_in_b[:, 4096:], ((0, 0), (0, pad))),
        "q_g": q_a_norm.reshape(1, Q_LORA),
        "kv_g": kv_a_norm.reshape(1, KV_LORA),
        "w_qn": wq[:, :, :MLA_NOPE_DIM].reshape(Q_LORA, NA_WIDTH),
        "w_qr": jnp.pad(wq[:, :, MLA_NOPE_DIM:], ((0, 0), (0, 0), (0, pad))).reshape(Q_LORA, NA_WIDTH),
        "w_kn": wkv[:, :, :MLA_NOPE_DIM].reshape(KV_LORA, NA_WIDTH),
        "w_v": wkv[:, :, MLA_NOPE_DIM:].reshape(KV_LORA, MLA_WIDTH),
    }


def _rope_tables():
    half = MLA_ROPE_DIM // 2
    inv_freq = np.float32(ROPE_THETA) ** (-np.arange(0, half, 2, dtype=np.float32) / np.float32(half))
    t = np.arange(DEC_SEQ, dtype=np.int32)

    def tables(pos):
        ang = pos.astype(np.float32)[:, None] * inv_freq
        s = np.sin(ang)
        return np.concatenate([np.cos(ang)] * 2, -1), np.concatenate([-s, s], -1)

    cr, sr = tables(t // GRID_W)
    cc, sc = tables(t % GRID_W)
    pad = LANES - MLA_ROPE_DIM
    cos = np.concatenate([cr, cc, np.ones((DEC_SEQ, pad), np.float32)], -1)
    sin = np.concatenate([sr, sc, np.zeros((DEC_SEQ, pad), np.float32)], -1)
    return jnp.asarray(cos, F32), jnp.asarray(sin, F32)


def _na_bias_plan():
    plan = []
    for step in NA_VARIANT_STEPS:
        start = min(max(NA_QROWS * step - NA_WIN_ROWS // 2, 0), GRID_H - NA_KROWS)
        per_row = []
        for i in range(NA_QROWS):
            r = NA_QROWS * step + i
            rs = min(max(r - NA_WIN_ROWS // 2, 0), GRID_H - NA_WIN_ROWS)
            per_row.append([(start + u - r + NA_WIN_ROWS - 1, rs <= start + u < rs + NA_WIN_ROWS)
                            for u in range(NA_KROWS)])
        plan.append(per_row)
    return plan


def _na_bias_kernel(ra_ref, rb_ref, o_ref):
    shape = (GRID_W, LANES)
    lane = lax.broadcasted_iota(jnp.int32, shape, 1)
    qc = lax.broadcasted_iota(jnp.int32, shape, 0)
    kc = lane % GRID_W
    cs = jnp.clip(qc - NA_WIN_COLS // 2, 0, GRID_W - NA_WIN_COLS)
    ok_col = (kc >= cs) & (kc < cs + NA_WIN_COLS)
    left = lane < GRID_W
    neg = jnp.full(shape, NEG, F32)
    for v, per_row in enumerate(_na_bias_plan()):
        for i, rows in enumerate(per_row):
            for t in range(len(rows) // 2):
                (dra, oka), (drb, okb) = rows[2 * t], rows[2 * t + 1]
                if oka and okb:
                    ok = ok_col
                elif oka:
                    ok = ok_col & left
                elif okb:
                    ok = ok_col & jnp.logical_not(left)
                for h in range(NA_HEADS):
                    if oka or okb:
                        src = jnp.zeros((1, LANES), F32)
                        if oka:
                            src = src + ra_ref[h, dra:dra + 1, :]
                        if okb:
                            src = src + rb_ref[h, drb:drb + 1, :]
                        val = pltpu.roll(jnp.broadcast_to(src, shape), LANES - (NA_WIN_COLS - 1), 1,
                                         stride=1, stride_axis=0)
                        tile = jnp.where(ok, val * LOG2E, neg)
                    else:
                        tile = neg
                    o_ref[v, h, i * GRID_W:(i + 1) * GRID_W, t * LANES:(t + 1) * LANES] = tile


def _na_bias_tables(rpb):
    n_dr, n_dc = 2 * NA_WIN_ROWS - 1, 2 * NA_WIN_COLS - 1
    ra = jnp.pad(rpb, ((0, 0), (0, 16 - n_dr), (0, LANES - n_dc)))
    rb = jnp.pad(rpb, ((0, 0), (0, 16 - n_dr), (GRID_W, LANES - GRID_W - n_dc)))
    return pl.pallas_call(
        _na_bias_kernel,
        out_shape=jax.ShapeDtypeStruct((len(NA_VARIANT_STEPS), NA_HEADS, NA_QTOK, NA_WIN_TOK), F32),
        compiler_params=pltpu.CompilerParams(vmem_limit_bytes=VMEM_LIMIT),
        name="na_bias_tables",
    )(ra, rb)


def kernel(x_prompt, x_sample, cache_na_k, cache_na_v, cache_mla_ckv, cache_mla_krope, c, c_ctx,
           w_mod, b_mod, w_in, q_a_norm, kv_a_norm, w_q_b, w_kv_b, na_rpb, w_o,
           ln1_g, ln1_b, w_gu, w_down, ln2_g, ln2_b):
    cond8 = jnp.concatenate([c_ctx[None], c, jnp.zeros((8 - 1 - DEC_BATCH, D_MODEL), F32)], 0)
    mod3 = _modulation(cond8, w_mod[0], b_mod[0][None]).reshape(8, 6, D_MODEL)

    w = _prep_weights(w_in[0], q_a_norm[0], kv_a_norm[0], w_q_b[0], w_kv_b[0])
    g1, b1 = ln1_g[0][None], ln1_b[0][None]
    g2, b2 = ln2_g[0][None], ln2_b[0][None]
    cos, sin = _rope_tables()

    xp = x_prompt.reshape(N_TOK, D_MODEL)
    xs = x_sample.reshape(N_TOK, D_MODEL)
    ctx_mod = lambda i, tm: 0
    lat_mod = lambda i, tm: 1 + i // (DEC_SEQ // tm)

    naq, nak, nav, lat, kr, st_k, st_v = _projection(xp, mod3, ctx_mod, w, state=True)
    qm, km, vm, st_ckv = _expand(lat, kr, w, rope=False, state=True)
    st_kr = kr[:, :MLA_ROPE_DIM]
    a_na, a_mla, w_o_b, w_gu_b = _ctx_attention(naq, nak, nav, qm, km, vm, (w_o[0], w_gu[0]))
    x1, w_down_b = _out_projection(a_na, a_mla, xp, mod3, ctx_mod, w_o_b, g1, b1, (w_down[0],))
    yp = _ffn(x1, mod3, ctx_mod, w_gu_b, w_down_b, g2, b2)

    naq, nak, nav, lat, kr = _projection(xs, mod3, lat_mod, w, state=False)
    qm, km, vm = _expand(lat, kr, w, rope=True, state=False, cos=cos, sin=sin)
    kr_cache = jnp.pad(cache_mla_krope[:, 0].reshape(DEC_BATCH * PAST_LEN, MLA_ROPE_DIM),
                       ((0, 0), (0, LANES - MLA_ROPE_DIM)))
    kmc, vmc = _cache_kv(cache_mla_ckv[:, 0].reshape(DEC_BATCH * PAST_LEN, KV_LORA), kr_cache, w)
    kc = cache_na_k[:, 0].reshape(DEC_BATCH * PAST_LEN, NA_WIDTH).astype(BF)
    vc = cache_na_v[:, 0].reshape(DEC_BATCH * PAST_LEN, NA_WIDTH).astype(BF)
    a_na = _lat_na_attention(naq, nak, nav, kc, vc, _na_bias_tables(na_rpb[0]))
    a_mla = _lat_mla_attention(qm, km, vm, kmc, vmc)
    x1, = _out_projection(a_na, a_mla, xs, mod3, lat_mod, w_o_b, g1, b1)
    ys = _ffn(x1, mod3, lat_mod, w_gu_b, w_down_b, g2, b2)

    return (yp.reshape(BATCH, SEQ, D_MODEL),
            ys.reshape(DEC_BATCH, DEC_SEQ, D_MODEL),
            st_k.reshape(BATCH, 1, SEQ, NA_HEADS, NA_HEAD_DIM),
            st_v.reshape(BATCH, 1, SEQ, NA_HEADS, NA_HEAD_DIM),
            st_ckv.reshape(BATCH, 1, SEQ, KV_LORA),
            st_kr.reshape(BATCH, 1, SEQ, MLA_ROPE_DIM))
```

```python
import functools

import numpy as np
import jax
import jax.numpy as jnp
from jax import lax
from jax.experimental import pallas as pl
from jax.experimental.pallas import tpu as pltpu

D_MODEL = 2048
BATCH = 32
SEQ = 256
DEC_BATCH = 2
DEC_SEQ = 4096
PAST_LEN = 512
GRID_W = 64
GRID_H = DEC_SEQ // GRID_W
NA_HEADS = 8
NA_HEAD_DIM = 128
NA_WIN_ROWS = 8
NA_WIN_COLS = 16
MLA_HEADS = 8
MLA_NOPE_DIM = 128
MLA_ROPE_DIM = 64
MLA_V_DIM = 128
Q_LORA = 512
KV_LORA = 512
NA_WIDTH = NA_HEADS * NA_HEAD_DIM
MLA_WIDTH = MLA_HEADS * MLA_V_DIM
D_FF = -(-8 * D_MODEL // (3 * 256)) * 256
ROPE_THETA = 10000.0
LN_EPS = 1e-5
RMS_EPS = 1e-6
ALPHA = 2.0 ** 0.25
LOG2E = 1.4426950408889634
NA_SCALE = NA_HEAD_DIM ** -0.5 * LOG2E
MLA_SCALE = (MLA_NOPE_DIM + MLA_ROPE_DIM) ** -0.5 * LOG2E

BF = jnp.bfloat16
F32 = jnp.float32
LANES = 128
MLA_QK_PAD = 2 * LANES
NEG = -1e30
VMEM_LIMIT = 60 * 1024 * 1024
FFN_VMEM_LIMIT = 63 * 1024 * 1024

N_TOK = BATCH * SEQ
assert N_TOK == DEC_BATCH * DEC_SEQ

MOD_TN = 768
PROJ_TM = 512
EXPAND_TM = 1024
CTX_BPS = 2
OPROJ_TM = 512
FFN_TM = 1024
FFN_TF = 512
FFN_TN = 512
MLA_TQ = 1024
MLA_KB = 512
MLA_HPS = 2
NA_QROWS = 4
NA_STEPS = GRID_H // NA_QROWS
NA_KROWS = NA_QROWS + NA_WIN_ROWS
NA_QTOK = NA_QROWS * GRID_W
NA_WIN_TOK = NA_KROWS * GRID_W
NA_VARIANT_STEPS = (0, 1, NA_STEPS - 1)
assert NA_QROWS == NA_WIN_ROWS // 2 and NA_KROWS % 2 == 0


def _cparams(sem, vmem_limit=VMEM_LIMIT):
    return pltpu.CompilerParams(dimension_semantics=sem, vmem_limit_bytes=vmem_limit)


def _dot(a, b):
    return jnp.dot(a, b, preferred_element_type=F32)


def _dot_nt(a, b):
    return lax.dot_general(a, b, (((1,), (1,)), ((), ())), preferred_element_type=F32)


def _sigmoid(x):
    return 1.0 / (1.0 + jnp.exp(-x))


def _layernorm(y, g, b):
    mu = jnp.mean(y, axis=-1, keepdims=True)
    yc = y - mu
    var = jnp.mean(yc * yc, axis=-1, keepdims=True)
    return yc * lax.rsqrt(var + LN_EPS) * g + b


def _rmsnorm(x, g):
    return x * lax.rsqrt(jnp.mean(x * x, axis=-1, keepdims=True) + RMS_EPS) * g


def _mod_kernel(c_ref, w_ref, b_ref, *rest):
    n_cast = (len(rest) - 1) // 2
    o_ref = rest[n_cast]
    _cast_rows(rest[:n_cast], rest[n_cast + 1:])
    c = c_ref[...]
    s = (c * _sigmoid(c)).astype(BF)
    o_ref[...] = _dot(s, w_ref[...].astype(BF)) + b_ref[...]


def _modulation(cond8, w_mod, b_mod, cast_weights=()):
    tn = MOD_TN
    n = 6 * D_MODEL
    c_in, c_out, c_shape = _cast_specs(cast_weights, n // tn)
    return pl.pallas_call(
        _mod_kernel,
        out_shape=[jax.ShapeDtypeStruct((8, n), F32)] + c_shape,
        grid=(n // tn,),
        in_specs=[pl.BlockSpec((8, D_MODEL), lambda j: (0, 0)),
                  pl.BlockSpec((D_MODEL, tn), lambda j: (0, j)),
                  pl.BlockSpec((1, tn), lambda j: (0, j))] + c_in,
        out_specs=[pl.BlockSpec((8, tn), lambda j: (0, j))] + c_out,
        compiler_params=_cparams(("parallel",)),
        name="modulation",
    )(cond8, w_mod, b_mod, *cast_weights)


def _rope128(x, cos, sin):
    lane = lax.broadcasted_iota(jnp.int32, x.shape, 1)
    partner = jnp.where(lane % 32 < 16, pltpu.roll(x, LANES - 16, 1), pltpu.roll(x, 16, 1))
    return x * cos + partner * sin


def _proj_kernel(x_ref, mod_ref, wm_ref, wr_ref, naq_ref, nak_ref, nav_ref, lat_ref, kr_ref, *state_refs):
    hb = (x_ref[...] * (1.0 + mod_ref[0, 1:2, :]) + mod_ref[0, 0:1, :]).astype(BF)
    outs = (naq_ref, nak_ref, nav_ref, lat_ref)
    for n, o_ref in enumerate(outs):
        z = _dot(hb, wm_ref[:, n * NA_WIDTH:(n + 1) * NA_WIDTH])
        if state_refs and n in (1, 2):
            state_refs[n - 1][...] = z.reshape(z.shape[0], NA_HEADS, NA_HEAD_DIM)
        if n == 0:
            z = z * NA_SCALE
        o_ref[...] = z.astype(o_ref.dtype)
    kr_ref[...] = _dot(hb, wr_ref[...])


def _expand_kernel(*refs, rope, state):
    it = iter(refs)
    lat_ref, kr_ref, qg_ref, kvg_ref, wqn_ref, wqr_ref, wkn_ref, wv_ref = (next(it) for _ in range(8))
    cos_ref, sin_ref = (next(it), next(it)) if rope else (None, None)
    qm_ref, km_ref, vm_ref = (next(it) for _ in range(3))
    ckv_ref = next(it) if state else None

    qn = _rmsnorm(lat_ref[:, :Q_LORA], qg_ref[...]).astype(BF)
    qnope = _dot(qn, wqn_ref[...]) * MLA_SCALE
    qrope = _dot(qn, wqr_ref[...]) * MLA_SCALE
    for h in range(MLA_HEADS):
        sl = slice(h * LANES, (h + 1) * LANES)
        r = qrope[:, sl]
        if rope:
            r = _rope128(r, cos_ref[...], sin_ref[...])
        qm_ref[:, h * MLA_QK_PAD: h * MLA_QK_PAD + LANES] = qnope[:, sl].astype(BF)
        qm_ref[:, h * MLA_QK_PAD + LANES: (h + 1) * MLA_QK_PAD] = r.astype(BF)

    ckv = _rmsnorm(lat_ref[:, Q_LORA:], kvg_ref[...])
    if state:
        ckv_ref[...] = ckv
    cb = ckv.astype(BF)
    kn = _dot(cb, wkn_ref[...])
    vm_ref[...] = _dot(cb, wv_ref[...]).astype(BF)
    kr = kr_ref[...]
    if rope:
        kr = _rope128(kr, cos_ref[...], sin_ref[...])
    krb = kr.astype(BF)
    for h in range(MLA_HEADS):
        km_ref[:, h * MLA_QK_PAD: h * MLA_QK_PAD + LANES] = kn[:, h * LANES:(h + 1) * LANES].astype(BF)
        km_ref[:, h * MLA_QK_PAD + LANES: (h + 1) * MLA_QK_PAD] = krb


def _projection(x2d, mod3, mod_map1, w, *, state):
    tm = PROJ_TM
    const = lambda i: (0, 0)
    row = lambda i: (i, 0)
    whole = pl.Buffered(1)
    out_shape = [jax.ShapeDtypeStruct((N_TOK, NA_WIDTH), BF)] * 3 + [
        jax.ShapeDtypeStruct((N_TOK, Q_LORA + KV_LORA), F32), jax.ShapeDtypeStruct((N_TOK, LANES), F32)]
    out_specs = [pl.BlockSpec((tm, NA_WIDTH), row)] * 3 + [
        pl.BlockSpec((tm, Q_LORA + KV_LORA), row), pl.BlockSpec((tm, LANES), row)]
    if state:
        out_shape += [jax.ShapeDtypeStruct((N_TOK, NA_HEADS, NA_HEAD_DIM), F32)] * 2
        out_specs += [pl.BlockSpec((tm, NA_HEADS, NA_HEAD_DIM), lambda i: (i, 0, 0))] * 2
    return pl.pallas_call(
        _proj_kernel,
        out_shape=out_shape,
        grid=(N_TOK // tm,),
        in_specs=[pl.BlockSpec((tm, D_MODEL), row),
                  pl.BlockSpec((1, 6, D_MODEL), lambda i: (mod_map1(i, tm), 0, 0)),
                  pl.BlockSpec((D_MODEL, 4 * NA_WIDTH), const, pipeline_mode=whole),
                  pl.BlockSpec((D_MODEL, LANES), const, pipeline_mode=whole)],
        out_specs=out_specs,
        compiler_params=_cparams(("parallel",)),
        name="input_projection",
    )(x2d, mod3, w["w_main"], w["w_rope"])


def _expand(lat, kr, w, *, rope, state, cos=None, sin=None):
    tm = EXPAND_TM
    tiles_per_batch = DEC_SEQ // tm
    const = lambda i: (0, 0)
    row = lambda i: (i, 0)
    wide = MLA_HEADS * MLA_QK_PAD
    in_specs = [pl.BlockSpec((tm, Q_LORA + KV_LORA), row), pl.BlockSpec((tm, LANES), row),
                pl.BlockSpec((1, Q_LORA), const), pl.BlockSpec((1, KV_LORA), const),
                pl.BlockSpec((Q_LORA, NA_WIDTH), const), pl.BlockSpec((Q_LORA, NA_WIDTH), const),
                pl.BlockSpec((KV_LORA, NA_WIDTH), const), pl.BlockSpec((KV_LORA, MLA_WIDTH), const)]
    args = [lat, kr, w["q_g"], w["kv_g"], w["w_qn"], w["w_qr"], w["w_kn"], w["w_v"]]
    if rope:
        in_specs += [pl.BlockSpec((tm, LANES), lambda i: (i % tiles_per_batch, 0))] * 2
        args += [cos, sin]
    out_shape = [jax.ShapeDtypeStruct((N_TOK, wide), BF), jax.ShapeDtypeStruct((N_TOK, wide), BF),
                 jax.ShapeDtypeStruct((N_TOK, MLA_WIDTH), BF)]
    out_specs = [pl.BlockSpec((tm, wide), row), pl.BlockSpec((tm, wide), row), pl.BlockSpec((tm, MLA_WIDTH), row)]
    if state:
        out_shape.append(jax.ShapeDtypeStruct((N_TOK, KV_LORA), F32))
        out_specs.append(pl.BlockSpec((tm, KV_LORA), row))
    return pl.pallas_call(
        functools.partial(_expand_kernel, rope=rope, state=state),
        out_shape=out_shape,
        grid=(N_TOK // tm,),
        in_specs=in_specs,
        out_specs=out_specs,
        compiler_params=_cparams(("parallel",)),
        name="lowrank_expand_latent" if rope else "lowrank_expand_context",
    )(*args)


def _cache_kv_kernel(ckv_ref, kr_ref, wkn_ref, wv_ref, km_ref, vm_ref):
    cb = ckv_ref[...].astype(BF)
    kn = _dot(cb, wkn_ref[...])
    vm_ref[...] = _dot(cb, wv_ref[...]).astype(BF)
    krb = kr_ref[...].astype(BF)
    for h in range(MLA_HEADS):
        km_ref[:, h * MLA_QK_PAD: h * MLA_QK_PAD + LANES] = kn[:, h * LANES:(h + 1) * LANES].astype(BF)
        km_ref[:, h * MLA_QK_PAD + LANES: (h + 1) * MLA_QK_PAD] = krb


def _cache_kv(ckv2d, kr2d, w):
    n = DEC_BATCH * PAST_LEN
    tm = PAST_LEN
    const = lambda i: (0, 0)
    row = lambda i: (i, 0)
    return pl.pallas_call(
        _cache_kv_kernel,
        out_shape=[jax.ShapeDtypeStruct((n, MLA_HEADS * MLA_QK_PAD), BF),
                   jax.ShapeDtypeStruct((n, MLA_WIDTH), BF)],
        grid=(n // tm,),
        in_specs=[pl.BlockSpec((tm, KV_LORA), row), pl.BlockSpec((tm, LANES), row),
                  pl.BlockSpec((KV_LORA, NA_WIDTH), const), pl.BlockSpec((KV_LORA, MLA_WIDTH), const)],
        out_specs=[pl.BlockSpec((tm, MLA_HEADS * MLA_QK_PAD), row), pl.BlockSpec((tm, MLA_WIDTH), row)],
        compiler_params=_cparams(("parallel",)),
        name="cache_kv_expand",
    )(ckv2d, kr2d, w["w_kn"], w["w_v"])


def _softmax_av(s_list, v_list):
    m = functools.reduce(jnp.maximum, [s.max(axis=-1, keepdims=True) for s in s_list])
    p_list = [jnp.exp2(s - m) for s in s_list]
    l = functools.reduce(jnp.add, [p.sum(axis=-1, keepdims=True) for p in p_list])
    o = functools.reduce(jnp.add, [_dot(p.astype(BF), v) for p, v in zip(p_list, v_list)])
    return o * (1.0 / l)


def _cast_rows(src_refs, dst_refs):
    for src, dst in zip(src_refs, dst_refs):
        dst[...] = src[...].astype(BF)


def _cast_specs(weights, steps, linear_step=lambda i: i):
    in_specs, out_specs, out_shape = [], [], []
    for wgt in weights:
        rows, cols = wgt.shape
        assert rows % (16 * steps) == 0
        blk = pl.BlockSpec((rows // steps, cols), lambda *g: (linear_step(*g), 0))
        in_specs.append(blk)
        out_specs.append(blk)
        out_shape.append(jax.ShapeDtypeStruct((rows, cols), BF))
    return in_specs, out_specs, out_shape


def _ctx_attn_kernel(q_ref, k_ref, v_ref, qm_ref, km_ref, vm_ref, *rest):
    n_cast = (len(rest) - 2) // 2
    ona_ref, omla_ref = rest[n_cast:n_cast + 2]
    _cast_rows(rest[:n_cast], rest[n_cast + 2:])
    for b in range(CTX_BPS):
        rows = slice(b * SEQ, (b + 1) * SEQ)
        for h in range(NA_HEADS):
            sl = slice(h * NA_HEAD_DIM, (h + 1) * NA_HEAD_DIM)
            s = _dot_nt(q_ref[rows, sl], k_ref[rows, sl])
            ona_ref[rows, sl] = _softmax_av([s], [v_ref[rows, sl]]).astype(BF)
        for h in range(MLA_HEADS):
            sq = slice(h * MLA_QK_PAD, (h + 1) * MLA_QK_PAD)
            sv = slice(h * MLA_V_DIM, (h + 1) * MLA_V_DIM)
            s = _dot_nt(qm_ref[rows, sq], km_ref[rows, sq])
            omla_ref[rows, sv] = _softmax_av([s], [vm_ref[rows, sv]]).astype(BF)


def _ctx_attention(naq, nak, nav, qm, km, vm, cast_weights):
    row = lambda b: (b, 0)
    wide = MLA_HEADS * MLA_QK_PAD
    tb = CTX_BPS * SEQ
    steps = BATCH // CTX_BPS
    c_in, c_out, c_shape = _cast_specs(cast_weights, steps)
    return pl.pallas_call(
        _ctx_attn_kernel,
        out_shape=[jax.ShapeDtypeStruct((N_TOK, NA_WIDTH), BF), jax.ShapeDtypeStruct((N_TOK, MLA_WIDTH), BF)]
                  + c_shape,
        grid=(steps,),
        in_specs=[pl.BlockSpec((tb, NA_WIDTH), row)] * 3
                 + [pl.BlockSpec((tb, wide), row)] * 2 + [pl.BlockSpec((tb, MLA_WIDTH), row)] + c_in,
        out_specs=[pl.BlockSpec((tb, NA_WIDTH), row), pl.BlockSpec((tb, MLA_WIDTH), row)] + c_out,
        compiler_params=_cparams(("parallel",)),
        name="context_attention",
    )(naq, nak, nav, qm, km, vm, *cast_weights)


def _na_window_start(step):
    return jnp.clip(NA_QROWS * step - NA_WIN_ROWS // 2, 0, GRID_H - NA_KROWS)


def _lat_na_kernel(q_ref, k_ref, v_ref, kc_ref, vc_ref, bias_ref, o_ref):
    start = pl.multiple_of(_na_window_start(pl.program_id(1)) * GRID_W, NA_QTOK)
    for h in range(NA_HEADS):
        sl = slice(h * NA_HEAD_DIM, (h + 1) * NA_HEAD_DIM)
        q = q_ref[:, sl]
        s_loc = _dot_nt(q, k_ref[pl.ds(start, NA_WIN_TOK), sl]) + bias_ref[0, h]
        s_ctx = _dot_nt(q, kc_ref[:, sl])
        o = _softmax_av([s_loc, s_ctx], [v_ref[pl.ds(start, NA_WIN_TOK), sl], vc_ref[:, sl]])
        o_ref[:, sl] = o.astype(BF)


def _na_bias_variant(step):
    return jnp.where(step == 0, 0, jnp.where(step == NA_STEPS - 1, 2, 1))


def _lat_na_attention(naq, nak, nav, kc, vc, bias):
    whole = pl.Buffered(1)
    return pl.pallas_call(
        _lat_na_kernel,
        out_shape=jax.ShapeDtypeStruct((N_TOK, NA_WIDTH), BF),
        grid=(DEC_BATCH, NA_STEPS),
        in_specs=[pl.BlockSpec((NA_QTOK, NA_WIDTH), lambda b, j: (b * NA_STEPS + j, 0)),
                  pl.BlockSpec((DEC_SEQ, NA_WIDTH), lambda b, j: (b, 0), pipeline_mode=whole),
                  pl.BlockSpec((DEC_SEQ, NA_WIDTH), lambda b, j: (b, 0), pipeline_mode=whole),
                  pl.BlockSpec((PAST_LEN, NA_WIDTH), lambda b, j: (b, 0)),
                  pl.BlockSpec((PAST_LEN, NA_WIDTH), lambda b, j: (b, 0)),
                  pl.BlockSpec((1, NA_HEADS, NA_QTOK, NA_WIN_TOK), lambda b, j: (_na_bias_variant(j), 0, 0, 0))],
        out_specs=pl.BlockSpec((NA_QTOK, NA_WIDTH), lambda b, j: (b * NA_STEPS + j, 0)),
        compiler_params=_cparams(("parallel", "arbitrary")),
        name="latent_neighbourhood_attention",
    )(naq, nak, nav, kc, vc, bias)


def _lat_mla_kernel(q_ref, k_ref, v_ref, kc_ref, vc_ref, *rest):
    n_cast = (len(rest) - 1) // 2
    o_ref = rest[n_cast]
    _cast_rows(rest[:n_cast], rest[n_cast + 1:])
    heads = range(MLA_HPS)
    qk = [slice(h * MLA_QK_PAD, (h + 1) * MLA_QK_PAD) for h in heads]
    vv = [slice(h * MLA_V_DIM, (h + 1) * MLA_V_DIM) for h in heads]
    q = [q_ref[:, qk[h]] for h in heads]
    m, l, acc = [], [], []
    for h in heads:
        s = _dot_nt(q[h], kc_ref[:, qk[h]])
        m.append(s.max(axis=-1, keepdims=True))
        p = jnp.exp2(s - m[h])
        l.append(p.sum(axis=-1, keepdims=True))
        acc.append(_dot(p.astype(BF), vc_ref[:, vv[h]]))
    for c in range(DEC_SEQ // MLA_KB):
        ks = slice(c * MLA_KB, (c + 1) * MLA_KB)
        for h in heads:
            s = _dot_nt(q[h], k_ref[ks, qk[h]])
            m_new = jnp.maximum(m[h], s.max(axis=-1, keepdims=True))
            a = jnp.exp2(m[h] - m_new)
            p = jnp.exp2(s - m_new)
            l[h] = a * l[h] + p.sum(axis=-1, keepdims=True)
            acc[h] = a * acc[h] + _dot(p.astype(BF), v_ref[ks, vv[h]])
            m[h] = m_new
    for h in heads:
        o_ref[:, vv[h]] = (acc[h] * (1.0 / l[h])).astype(BF)


def _lat_mla_attention(qm, km, vm, kmc, vmc, cast_weights=()):
    nq = DEC_SEQ // MLA_TQ
    nh = MLA_HEADS // MLA_HPS
    qw, vw = MLA_HPS * MLA_QK_PAD, MLA_HPS * MLA_V_DIM
    c_in, c_out, c_shape = _cast_specs(cast_weights, DEC_BATCH * nh * nq, lambda b, h, i: (b * nh + h) * nq + i)
    return pl.pallas_call(
        _lat_mla_kernel,
        out_shape=[jax.ShapeDtypeStruct((N_TOK, MLA_WIDTH), BF)] + c_shape,
        grid=(DEC_BATCH, nh, nq),
        in_specs=[pl.BlockSpec((MLA_TQ, qw), lambda b, h, i: (b * nq + i, h)),
                  pl.BlockSpec((DEC_SEQ, qw), lambda b, h, i: (b, h)),
                  pl.BlockSpec((DEC_SEQ, vw), lambda b, h, i: (b, h)),
                  pl.BlockSpec((PAST_LEN, qw), lambda b, h, i: (b, h)),
                  pl.BlockSpec((PAST_LEN, vw), lambda b, h, i: (b, h))] + c_in,
        out_specs=[pl.BlockSpec((MLA_TQ, vw), lambda b, h, i: (b * nq + i, h))] + c_out,
        compiler_params=_cparams(("arbitrary", "arbitrary", "arbitrary")),
        name="latent_mla_attention",
    )(qm, km, vm, kmc, vmc, *cast_weights)


def _oproj_kernel(ana_ref, amla_ref, x_ref, mod_ref, w1_ref, w2_ref, g_ref, b_ref, *rest):
    n_cast = (len(rest) - 1) // 2
    o_ref = rest[n_cast]
    _cast_rows(rest[:n_cast], rest[n_cast + 1:])
    o = _dot(ana_ref[...], w1_ref[...]) + _dot(amla_ref[...], w2_ref[...])
    y = ALPHA * x_ref[...] + mod_ref[0, 2:3, :] * o
    o_ref[...] = _layernorm(y, g_ref[...], b_ref[...])


def _out_projection(ana, amla, x2d, mod3, mod_map1, w_o, g, b, cast_weights=()):
    assert NA_WIDTH == MLA_WIDTH
    tm = OPROJ_TM
    steps = N_TOK // tm
    row = lambda i: (i, 0)
    const = lambda i: (0, 0)
    whole = pl.Buffered(1)
    c_in, c_out, c_shape = _cast_specs(cast_weights, steps)
    return pl.pallas_call(
        _oproj_kernel,
        out_shape=[jax.ShapeDtypeStruct((N_TOK, D_MODEL), F32)] + c_shape,
        grid=(steps,),
        in_specs=[pl.BlockSpec((tm, NA_WIDTH), row), pl.BlockSpec((tm, MLA_WIDTH), row),
                  pl.BlockSpec((tm, D_MODEL), row),
                  pl.BlockSpec((1, 6, D_MODEL), lambda i: (mod_map1(i, tm), 0, 0)),
                  pl.BlockSpec((NA_WIDTH, D_MODEL), const, pipeline_mode=whole),
                  pl.BlockSpec((MLA_WIDTH, D_MODEL), lambda i: (1, 0), pipeline_mode=whole),
                  pl.BlockSpec((1, D_MODEL), const), pl.BlockSpec((1, D_MODEL), const)] + c_in,
        out_specs=[pl.BlockSpec((tm, D_MODEL), row)] + c_out,
        compiler_params=_cparams(("parallel",)),
        name="out_projection_ln1",
    )(ana, amla, x2d, mod3, w_o, w_o, g, b, *cast_weights)


def _ffn_kernel(x_ref, mod_ref, wg_ref, wu_ref, wd_ref, g_ref, b_ref, o_ref, h_scr):
    j = pl.program_id(1)

    @pl.when(j == 0)
    def _():
        h_scr[...] = (x_ref[...] * (1.0 + mod_ref[0, 4:5, :]) + mod_ref[0, 3:4, :]).astype(BF)
        o_ref[...] = jnp.zeros_like(o_ref)

    hb = h_scr[...]
    gate = _dot(hb, wg_ref[...])
    up = _dot(hb, wu_ref[...])
    act = (gate * _sigmoid(gate) * up).astype(BF)
    for n in range(D_MODEL // FFN_TN):
        sl = slice(n * FFN_TN, (n + 1) * FFN_TN)
        o_ref[:, sl] += _dot(act, wd_ref[:, sl])

    @pl.when(j == pl.num_programs(1) - 1)
    def _():
        y = ALPHA * x_ref[...] + mod_ref[0, 5:6, :] * o_ref[...]
        o_ref[...] = _layernorm(y, g_ref[...], b_ref[...])


def _ffn(x1, mod3, mod_map1, w_gu, w_down, g, b):
    tm, tf = FFN_TM, FFN_TF
    nf = D_FF // tf
    const = lambda i, j: (0, 0)
    row = lambda i, j: (i, 0)
    return pl.pallas_call(
        _ffn_kernel,
        out_shape=jax.ShapeDtypeStruct((N_TOK, D_MODEL), F32),
        grid=(N_TOK // tm, nf),
        in_specs=[pl.BlockSpec((tm, D_MODEL), row),
                  pl.BlockSpec((1, 6, D_MODEL), lambda i, j: (mod_map1(i, tm), 0, 0)),
                  pl.BlockSpec((D_MODEL, tf), lambda i, j: (0, j)),
                  pl.BlockSpec((D_MODEL, tf), lambda i, j: (0, j + nf)),
                  pl.BlockSpec((tf, D_MODEL), lambda i, j: (j, 0)),
                  pl.BlockSpec((1, D_MODEL), const), pl.BlockSpec((1, D_MODEL), const)],
        out_specs=pl.BlockSpec((tm, D_MODEL), row),
        scratch_shapes=[pltpu.VMEM((tm, D_MODEL), BF)],
        compiler_params=_cparams(("parallel", "arbitrary"), FFN_VMEM_LIMIT),
        name="swiglu_ln2",
    )(x1, mod3, w_gu, w_gu, w_down, g, b)


def _prep_weights(w_in_b, q_a_norm, kv_a_norm, w_q_b, w_kv_b):
    wq = w_q_b.astype(BF).reshape(Q_LORA, MLA_HEADS, MLA_NOPE_DIM + MLA_ROPE_DIM)
    wkv = w_kv_b.astype(BF).reshape(KV_LORA, MLA_HEADS, MLA_NOPE_DIM + MLA_V_DIM)
    pad = LANES - MLA_ROPE_DIM
    return {
        "w_main": w_in_b,
        "w_rope": jnp.pad(w_in_b[:, 4096:], ((0, 0), (0, pad))),
        "q_g": q_a_norm.reshape(1, Q_LORA),
        "kv_g": kv_a_norm.reshape(1, KV_LORA),
        "w_qn": wq[:, :, :MLA_NOPE_DIM].reshape(Q_LORA, NA_WIDTH),
        "w_qr": jnp.pad(wq[:, :, MLA_NOPE_DIM:], ((0, 0), (0, 0), (0, pad))).reshape(Q_LORA, NA_WIDTH),
        "w_kn": wkv[:, :, :MLA_NOPE_DIM].reshape(KV_LORA, NA_WIDTH),
        "w_v": wkv[:, :, MLA_NOPE_DIM:].reshape(KV_LORA, MLA_WIDTH),
    }


def _rope_tables():
    half = MLA_ROPE_DIM // 2
    inv_freq = np.float32(ROPE_THETA) ** (-np.arange(0, half, 2, dtype=np.float32) / np.float32(half))
    t = np.arange(DEC_SEQ, dtype=np.int32)

    def tables(pos):
        ang = pos.astype(np.float32)[:, None] * inv_freq
        s = np.sin(ang)
        return np.concatenate([np.cos(ang)] * 2, -1), np.concatenate([-s, s], -1)

    cr, sr = tables(t // GRID_W)
    cc, sc = tables(t % GRID_W)
    pad = LANES - MLA_ROPE_DIM
    cos = np.concatenate([cr, cc, np.ones((DEC_SEQ, pad), np.float32)], -1)
    sin = np.concatenate([sr, sc, np.zeros((DEC_SEQ, pad), np.float32)], -1)
    return jnp.asarray(cos, F32), jnp.asarray(sin, F32)


def _na_bias_plan():
    plan = []
    for step in NA_VARIANT_STEPS:
        start = min(max(NA_QROWS * step - NA_WIN_ROWS // 2, 0), GRID_H - NA_KROWS)
        per_row = []
        for i in range(NA_QROWS):
            r = NA_QROWS * step + i
            rs = min(max(r - NA_WIN_ROWS // 2, 0), GRID_H - NA_WIN_ROWS)
            per_row.append([(start + u - r + NA_WIN_ROWS - 1, rs <= start + u < rs + NA_WIN_ROWS)
                            for u in range(NA_KROWS)])
        plan.append(per_row)
    return plan


def _na_bias_kernel(ra_ref, rb_ref, o_ref):
    shape = (GRID_W, LANES)
    lane = lax.broadcasted_iota(jnp.int32, shape, 1)
    qc = lax.broadcasted_iota(jnp.int32, shape, 0)
    kc = lane % GRID_W
    cs = jnp.clip(qc - NA_WIN_COLS // 2, 0, GRID_W - NA_WIN_COLS)
    ok_col = (kc >= cs) & (kc < cs + NA_WIN_COLS)
    left = lane < GRID_W
    neg = jnp.full(shape, NEG, F32)
    for v, per_row in enumerate(_na_bias_plan()):
        for i, rows in enumerate(per_row):
            for t in range(len(rows) // 2):
                (dra, oka), (drb, okb) = rows[2 * t], rows[2 * t + 1]
                if oka and okb:
                    ok = ok_col
                elif oka:
                    ok = ok_col & left
                elif okb:
                    ok = ok_col & jnp.logical_not(left)
                for h in range(NA_HEADS):
                    if oka or okb:
                        src = jnp.zeros((1, LANES), F32)
                        if oka:
                            src = src + ra_ref[h, dra:dra + 1, :]
                        if okb:
                            src = src + rb_ref[h, drb:drb + 1, :]
                        val = pltpu.roll(jnp.broadcast_to(src, shape), LANES - (NA_WIN_COLS - 1), 1,
                                         stride=1, stride_axis=0)
                        tile = jnp.where(ok, val * LOG2E, neg)
                    else:
                        tile = neg
                    o_ref[v, h, i * GRID_W:(i + 1) * GRID_W, t * LANES:(t + 1) * LANES] = tile


def _na_bias_tables(rpb):
    n_dr, n_dc = 2 * NA_WIN_ROWS - 1, 2 * NA_WIN_COLS - 1
    ra = jnp.pad(rpb, ((0, 0), (0, 16 - n_dr), (0, LANES - n_dc)))
    rb = jnp.pad(rpb, ((0, 0), (0, 16 - n_dr), (GRID_W, LANES - GRID_W - n_dc)))
    return pl.pallas_call(
        _na_bias_kernel,
        out_shape=jax.ShapeDtypeStruct((len(NA_VARIANT_STEPS), NA_HEADS, NA_QTOK, NA_WIN_TOK), F32),
        compiler_params=pltpu.CompilerParams(vmem_limit_bytes=VMEM_LIMIT),
        name="na_bias_tables",
    )(ra, rb)


def kernel(x_prompt, x_sample, cache_na_k, cache_na_v, cache_mla_ckv, cache_mla_krope, c, c_ctx,
           w_mod, b_mod, w_in, q_a_norm, kv_a_norm, w_q_b, w_kv_b, na_rpb, w_o,
           ln1_g, ln1_b, w_gu, w_down, ln2_g, ln2_b):
    cond8 = jnp.concatenate([c_ctx[None], c, jnp.zeros((8 - 1 - DEC_BATCH, D_MODEL), F32)], 0)
    mod, w_in_b = _modulation(cond8, w_mod[0], b_mod[0][None], (w_in[0],))
    mod3 = mod.reshape(8, 6, D_MODEL)

    w = _prep_weights(w_in_b, q_a_norm[0], kv_a_norm[0], w_q_b[0], w_kv_b[0])
    g1, b1 = ln1_g[0][None], ln1_b[0][None]
    g2, b2 = ln2_g[0][None], ln2_b[0][None]
    cos, sin = _rope_tables()

    xp = x_prompt.reshape(N_TOK, D_MODEL)
    xs = x_sample.reshape(N_TOK, D_MODEL)
    ctx_mod = lambda i, tm: 0
    lat_mod = lambda i, tm: 1 + i // (DEC_SEQ // tm)

    naq, nak, nav, lat, kr, st_k, st_v = _projection(xp, mod3, ctx_mod, w, state=True)
    qm, km, vm, st_ckv = _expand(lat, kr, w, rope=False, state=True)
    st_kr = kr[:, :MLA_ROPE_DIM]
    a_na, a_mla, w_o_b = _ctx_attention(naq, nak, nav, qm, km, vm, (w_o[0],))
    x1_ctx, w_down_b = _out_projection(a_na, a_mla, xp, mod3, ctx_mod, w_o_b, g1, b1, (w_down[0],))

    naq, nak, nav, lat, kr = _projection(xs, mod3, lat_mod, w, state=False)
    qm, km, vm = _expand(lat, kr, w, rope=True, state=False, cos=cos, sin=sin)
    kr_cache = jnp.pad(cache_mla_krope[:, 0].reshape(DEC_BATCH * PAST_LEN, MLA_ROPE_DIM),
                       ((0, 0), (0, LANES - MLA_ROPE_DIM)))
    kmc, vmc = _cache_kv(cache_mla_ckv[:, 0].reshape(DEC_BATCH * PAST_LEN, KV_LORA), kr_cache, w)
    kc = cache_na_k[:, 0].reshape(DEC_BATCH * PAST_LEN, NA_WIDTH).astype(BF)
    vc = cache_na_v[:, 0].reshape(DEC_BATCH * PAST_LEN, NA_WIDTH).astype(BF)
    a_na = _lat_na_attention(naq, nak, nav, kc, vc, _na_bias_tables(na_rpb[0]))
    a_mla, w_gu_b = _lat_mla_attention(qm, km, vm, kmc, vmc, (w_gu[0],))
    x1_lat, = _out_projection(a_na, a_mla, xs, mod3, lat_mod, w_o_b, g1, b1)
    yp = _ffn(x1_ctx, mod3, ctx_mod, w_gu_b, w_down_b, g2, b2)
    ys = _ffn(x1_lat, mod3, lat_mod, w_gu_b, w_down_b, g2, b2)

    return (yp.reshape(BATCH, SEQ, D_MODEL),
            ys.reshape(DEC_BATCH, DEC_SEQ, D_MODEL),
            st_k.reshape(BATCH, 1, SEQ, NA_HEADS, NA_HEAD_DIM),
            st_v.reshape(BATCH, 1, SEQ, NA_HEADS, NA_HEAD_DIM),
            st_ckv.reshape(BATCH, 1, SEQ, KV_LORA),
            st_kr.reshape(BATCH, 1, SEQ, MLA_ROPE_DIM))
```

```python
import functools

import numpy as np
import jax
import jax.numpy as jnp
from jax import lax
from jax.experimental import pallas as pl
from jax.experimental.pallas import tpu as pltpu

D_MODEL = 2048
BATCH = 32
SEQ = 256
DEC_BATCH = 2
DEC_SEQ = 4096
PAST_LEN = 512
GRID_W = 64
GRID_H = DEC_SEQ // GRID_W
NA_HEADS = 8
NA_HEAD_DIM = 128
NA_WIN_ROWS = 8
NA_WIN_COLS = 16
MLA_HEADS = 8
MLA_NOPE_DIM = 128
MLA_ROPE_DIM = 64
MLA_V_DIM = 128
Q_LORA = 512
KV_LORA = 512
NA_WIDTH = NA_HEADS * NA_HEAD_DIM
MLA_WIDTH = MLA_HEADS * MLA_V_DIM
D_FF = -(-8 * D_MODEL // (3 * 256)) * 256
ROPE_THETA = 10000.0
LN_EPS = 1e-5
RMS_EPS = 1e-6
ALPHA = 2.0 ** 0.25
LOG2E = 1.4426950408889634
NA_SCALE = NA_HEAD_DIM ** -0.5 * LOG2E
MLA_SCALE = (MLA_NOPE_DIM + MLA_ROPE_DIM) ** -0.5 * LOG2E

BF = jnp.bfloat16
F32 = jnp.float32
LANES = 128
MLA_QK_PAD = 2 * LANES
NEG = -1e30
VMEM_LIMIT = 60 * 1024 * 1024
FFN_VMEM_LIMIT = 63 * 1024 * 1024

N_TOK = BATCH * SEQ
assert N_TOK == DEC_BATCH * DEC_SEQ

MOD_TN = 768
PROJ_TM = 512
EXPAND_TM = 1024
CTX_BPS = 2
OPROJ_TM = 512
FFN_TM = 1024
FFN_TF = 512
FFN_TN = 512
MLA_TQ = 1024
MLA_KB = 512
MLA_HPS = 2
NA_QROWS = 4
NA_STEPS = GRID_H // NA_QROWS
NA_KROWS = NA_QROWS + NA_WIN_ROWS
NA_QTOK = NA_QROWS * GRID_W
NA_WIN_TOK = NA_KROWS * GRID_W
NA_VARIANT_STEPS = (0, 1, NA_STEPS - 1)
assert NA_QROWS == NA_WIN_ROWS // 2 and NA_KROWS % 2 == 0


def _cparams(sem, vmem_limit=VMEM_LIMIT):
    return pltpu.CompilerParams(dimension_semantics=sem, vmem_limit_bytes=vmem_limit)


def _dot(a, b):
    return jnp.dot(a, b, preferred_element_type=F32)


def _dot_nt(a, b):
    return lax.dot_general(a, b, (((1,), (1,)), ((), ())), preferred_element_type=F32)


def _sigmoid(x):
    return 1.0 / (1.0 + jnp.exp(-x))


def _layernorm(y, g, b):
    mu = jnp.mean(y, axis=-1, keepdims=True)
    yc = y - mu
    var = jnp.mean(yc * yc, axis=-1, keepdims=True)
    return yc * lax.rsqrt(var + LN_EPS) * g + b


def _rmsnorm(x, g):
    return x * lax.rsqrt(jnp.mean(x * x, axis=-1, keepdims=True) + RMS_EPS) * g


def _mod_kernel(c_ref, w_ref, b_ref, wt_ref, wtr_ref, o_ref, wb_ref, wrb_ref):
    wb_ref[...] = wt_ref[...].T.astype(BF)
    rope_cols = wtr_ref[...].T.astype(BF)
    wrb_ref[...] = jnp.concatenate([rope_cols, jnp.zeros_like(rope_cols)], axis=1)
    c = c_ref[...]
    s = (c * _sigmoid(c)).astype(BF)
    o_ref[...] = _dot(s, w_ref[...].astype(BF)) + b_ref[...]


def _modulation(cond8, w_mod, b_mod, w_in_t):
    tn = MOD_TN
    n = 6 * D_MODEL
    steps = n // tn
    main = 4 * NA_WIDTH
    cols = main // steps
    return pl.pallas_call(
        _mod_kernel,
        out_shape=[jax.ShapeDtypeStruct((8, n), F32), jax.ShapeDtypeStruct((D_MODEL, main), BF),
                   jax.ShapeDtypeStruct((D_MODEL, LANES), BF)],
        grid=(steps,),
        in_specs=[pl.BlockSpec((8, D_MODEL), lambda j: (0, 0)),
                  pl.BlockSpec((D_MODEL, tn), lambda j: (0, j)),
                  pl.BlockSpec((1, tn), lambda j: (0, j)),
                  pl.BlockSpec((cols, D_MODEL), lambda j: (j, 0)),
                  pl.BlockSpec((MLA_ROPE_DIM, D_MODEL), lambda j: (main // MLA_ROPE_DIM, 0))],
        out_specs=[pl.BlockSpec((8, tn), lambda j: (0, j)), pl.BlockSpec((D_MODEL, cols), lambda j: (0, j)),
                   pl.BlockSpec((D_MODEL, LANES), lambda j: (0, 0))],
        compiler_params=_cparams(("arbitrary",)),
        name="modulation",
    )(cond8, w_mod, b_mod, w_in_t, w_in_t)


def _rope128(x, cos, sin):
    lane = lax.broadcasted_iota(jnp.int32, x.shape, 1)
    partner = jnp.where(lane % 32 < 16, pltpu.roll(x, LANES - 16, 1), pltpu.roll(x, 16, 1))
    return x * cos + partner * sin


def _proj_kernel(x_ref, mod_ref, wm_ref, wr_ref, naq_ref, nak_ref, nav_ref, lat_ref, kr_ref, *state_refs):
    hb = (x_ref[...] * (1.0 + mod_ref[0, 1:2, :]) + mod_ref[0, 0:1, :]).astype(BF)
    outs = (naq_ref, nak_ref, nav_ref, lat_ref)
    for n, o_ref in enumerate(outs):
        z = _dot(hb, wm_ref[:, n * NA_WIDTH:(n + 1) * NA_WIDTH])
        if state_refs and n in (1, 2):
            state_refs[n - 1][...] = z.reshape(z.shape[0], NA_HEADS, NA_HEAD_DIM)
        if n == 0:
            z = z * NA_SCALE
        o_ref[...] = z.astype(o_ref.dtype)
    kr_ref[...] = _dot(hb, wr_ref[...])


def _expand_kernel(*refs, rope, state):
    it = iter(refs)
    lat_ref, kr_ref, qg_ref, kvg_ref, wqn_ref, wqr_ref, wkn_ref, wv_ref = (next(it) for _ in range(8))
    cos_ref, sin_ref = (next(it), next(it)) if rope else (None, None)
    qm_ref, km_ref, vm_ref = (next(it) for _ in range(3))
    ckv_ref = next(it) if state else None

    qn = _rmsnorm(lat_ref[:, :Q_LORA], qg_ref[...]).astype(BF)
    qnope = _dot(qn, wqn_ref[...]) * MLA_SCALE
    qrope = _dot(qn, wqr_ref[...]) * MLA_SCALE
    for h in range(MLA_HEADS):
        sl = slice(h * LANES, (h + 1) * LANES)
        r = qrope[:, sl]
        if rope:
            r = _rope128(r, cos_ref[...], sin_ref[...])
        qm_ref[:, h * MLA_QK_PAD: h * MLA_QK_PAD + LANES] = qnope[:, sl].astype(BF)
        qm_ref[:, h * MLA_QK_PAD + LANES: (h + 1) * MLA_QK_PAD] = r.astype(BF)

    ckv = _rmsnorm(lat_ref[:, Q_LORA:], kvg_ref[...])
    if state:
        ckv_ref[...] = ckv
    cb = ckv.astype(BF)
    kn = _dot(cb, wkn_ref[...])
    vm_ref[...] = _dot(cb, wv_ref[...]).astype(BF)
    kr = kr_ref[...]
    if rope:
        kr = _rope128(kr, cos_ref[...], sin_ref[...])
    krb = kr.astype(BF)
    for h in range(MLA_HEADS):
        km_ref[:, h * MLA_QK_PAD: h * MLA_QK_PAD + LANES] = kn[:, h * LANES:(h + 1) * LANES].astype(BF)
        km_ref[:, h * MLA_QK_PAD + LANES: (h + 1) * MLA_QK_PAD] = krb


def _projection(x2d, mod3, mod_map1, w, *, state):
    tm = PROJ_TM
    const = lambda i: (0, 0)
    row = lambda i: (i, 0)
    whole = pl.Buffered(1)
    out_shape = [jax.ShapeDtypeStruct((N_TOK, NA_WIDTH), BF)] * 3 + [
        jax.ShapeDtypeStruct((N_TOK, Q_LORA + KV_LORA), F32), jax.ShapeDtypeStruct((N_TOK, LANES), F32)]
    out_specs = [pl.BlockSpec((tm, NA_WIDTH), row)] * 3 + [
        pl.BlockSpec((tm, Q_LORA + KV_LORA), row), pl.BlockSpec((tm, LANES), row)]
    if state:
        out_shape += [jax.ShapeDtypeStruct((N_TOK, NA_HEADS, NA_HEAD_DIM), F32)] * 2
        out_specs += [pl.BlockSpec((tm, NA_HEADS, NA_HEAD_DIM), lambda i: (i, 0, 0))] * 2
    return pl.pallas_call(
        _proj_kernel,
        out_shape=out_shape,
        grid=(N_TOK // tm,),
        in_specs=[pl.BlockSpec((tm, D_MODEL), row),
                  pl.BlockSpec((1, 6, D_MODEL), lambda i: (mod_map1(i, tm), 0, 0)),
                  pl.BlockSpec((D_MODEL, 4 * NA_WIDTH), const, pipeline_mode=whole),
                  pl.BlockSpec((D_MODEL, LANES), const, pipeline_mode=whole)],
        out_specs=out_specs,
        compiler_params=_cparams(("parallel",)),
        name="input_projection",
    )(x2d, mod3, w["w_main"], w["w_rope"])


def _expand(lat, kr, w, *, rope, state, cos=None, sin=None):
    tm = EXPAND_TM
    tiles_per_batch = DEC_SEQ // tm
    const = lambda i: (0, 0)
    row = lambda i: (i, 0)
    wide = MLA_HEADS * MLA_QK_PAD
    in_specs = [pl.BlockSpec((tm, Q_LORA + KV_LORA), row), pl.BlockSpec((tm, LANES), row),
                pl.BlockSpec((1, Q_LORA), const), pl.BlockSpec((1, KV_LORA), const),
                pl.BlockSpec((Q_LORA, NA_WIDTH), const), pl.BlockSpec((Q_LORA, NA_WIDTH), const),
                pl.BlockSpec((KV_LORA, NA_WIDTH), const), pl.BlockSpec((KV_LORA, MLA_WIDTH), const)]
    args = [lat, kr, w["q_g"], w["kv_g"], w["w_qn"], w["w_qr"], w["w_kn"], w["w_v"]]
    if rope:
        in_specs += [pl.BlockSpec((tm, LANES), lambda i: (i % tiles_per_batch, 0))] * 2
        args += [cos, sin]
    out_shape = [jax.ShapeDtypeStruct((N_TOK, wide), BF), jax.ShapeDtypeStruct((N_TOK, wide), BF),
                 jax.ShapeDtypeStruct((N_TOK, MLA_WIDTH), BF)]
    out_specs = [pl.BlockSpec((tm, wide), row), pl.BlockSpec((tm, wide), row), pl.BlockSpec((tm, MLA_WIDTH), row)]
    if state:
        out_shape.append(jax.ShapeDtypeStruct((N_TOK, KV_LORA), F32))
        out_specs.append(pl.BlockSpec((tm, KV_LORA), row))
    return pl.pallas_call(
        functools.partial(_expand_kernel, rope=rope, state=state),
        out_shape=out_shape,
        grid=(N_TOK // tm,),
        in_specs=in_specs,
        out_specs=out_specs,
        compiler_params=_cparams(("parallel",)),
        name="lowrank_expand_latent" if rope else "lowrank_expand_context",
    )(*args)


def _cache_kv_kernel(ckv_ref, kr_ref, wkn_ref, wv_ref, km_ref, vm_ref):
    cb = ckv_ref[...].astype(BF)
    kn = _dot(cb, wkn_ref[...])
    vm_ref[...] = _dot(cb, wv_ref[...]).astype(BF)
    krb = kr_ref[...].astype(BF)
    for h in range(MLA_HEADS):
        km_ref[:, h * MLA_QK_PAD: h * MLA_QK_PAD + LANES] = kn[:, h * LANES:(h + 1) * LANES].astype(BF)
        km_ref[:, h * MLA_QK_PAD + LANES: (h + 1) * MLA_QK_PAD] = krb


def _cache_kv(ckv2d, kr2d, w):
    n = DEC_BATCH * PAST_LEN
    tm = PAST_LEN
    const = lambda i: (0, 0)
    row = lambda i: (i, 0)
    return pl.pallas_call(
        _cache_kv_kernel,
        out_shape=[jax.ShapeDtypeStruct((n, MLA_HEADS * MLA_QK_PAD), BF),
                   jax.ShapeDtypeStruct((n, MLA_WIDTH), BF)],
        grid=(n // tm,),
        in_specs=[pl.BlockSpec((tm, KV_LORA), row), pl.BlockSpec((tm, LANES), row),
                  pl.BlockSpec((KV_LORA, NA_WIDTH), const), pl.BlockSpec((KV_LORA, MLA_WIDTH), const)],
        out_specs=[pl.BlockSpec((tm, MLA_HEADS * MLA_QK_PAD), row), pl.BlockSpec((tm, MLA_WIDTH), row)],
        compiler_params=_cparams(("parallel",)),
        name="cache_kv_expand",
    )(ckv2d, kr2d, w["w_kn"], w["w_v"])


def _softmax_av(s_list, v_list):
    m = functools.reduce(jnp.maximum, [s.max(axis=-1, keepdims=True) for s in s_list])
    p_list = [jnp.exp2(s - m) for s in s_list]
    l = functools.reduce(jnp.add, [p.sum(axis=-1, keepdims=True) for p in p_list])
    o = functools.reduce(jnp.add, [_dot(p.astype(BF), v) for p, v in zip(p_list, v_list)])
    return o * (1.0 / l)


def _cast_rows(src_refs, dst_refs):
    for src, dst in zip(src_refs, dst_refs):
        dst[...] = src[...].astype(BF)


def _cast_specs(weights, steps, linear_step=lambda i: i):
    in_specs, out_specs, out_shape = [], [], []
    for wgt in weights:
        rows, cols = wgt.shape
        assert rows % (16 * steps) == 0
        blk = pl.BlockSpec((rows // steps, cols), lambda *g: (linear_step(*g), 0))
        in_specs.append(blk)
        out_specs.append(blk)
        out_shape.append(jax.ShapeDtypeStruct((rows, cols), BF))
    return in_specs, out_specs, out_shape


def _ctx_attn_kernel(q_ref, k_ref, v_ref, qm_ref, km_ref, vm_ref, *rest):
    n_cast = (len(rest) - 2) // 2
    ona_ref, omla_ref = rest[n_cast:n_cast + 2]
    _cast_rows(rest[:n_cast], rest[n_cast + 2:])
    for b in range(CTX_BPS):
        rows = slice(b * SEQ, (b + 1) * SEQ)
        for h in range(NA_HEADS):
            sl = slice(h * NA_HEAD_DIM, (h + 1) * NA_HEAD_DIM)
            s = _dot_nt(q_ref[rows, sl], k_ref[rows, sl])
            ona_ref[rows, sl] = _softmax_av([s], [v_ref[rows, sl]]).astype(BF)
        for h in range(MLA_HEADS):
            sq = slice(h * MLA_QK_PAD, (h + 1) * MLA_QK_PAD)
            sv = slice(h * MLA_V_DIM, (h + 1) * MLA_V_DIM)
            s = _dot_nt(qm_ref[rows, sq], km_ref[rows, sq])
            omla_ref[rows, sv] = _softmax_av([s], [vm_ref[rows, sv]]).astype(BF)


def _ctx_attention(naq, nak, nav, qm, km, vm, cast_weights):
    row = lambda b: (b, 0)
    wide = MLA_HEADS * MLA_QK_PAD
    tb = CTX_BPS * SEQ
    steps = BATCH // CTX_BPS
    c_in, c_out, c_shape = _cast_specs(cast_weights, steps)
    return pl.pallas_call(
        _ctx_attn_kernel,
        out_shape=[jax.ShapeDtypeStruct((N_TOK, NA_WIDTH), BF), jax.ShapeDtypeStruct((N_TOK, MLA_WIDTH), BF)]
                  + c_shape,
        grid=(steps,),
        in_specs=[pl.BlockSpec((tb, NA_WIDTH), row)] * 3
                 + [pl.BlockSpec((tb, wide), row)] * 2 + [pl.BlockSpec((tb, MLA_WIDTH), row)] + c_in,
        out_specs=[pl.BlockSpec((tb, NA_WIDTH), row), pl.BlockSpec((tb, MLA_WIDTH), row)] + c_out,
        compiler_params=_cparams(("parallel",)),
        name="context_attention",
    )(naq, nak, nav, qm, km, vm, *cast_weights)


def _na_window_start(step):
    return jnp.clip(NA_QROWS * step - NA_WIN_ROWS // 2, 0, GRID_H - NA_KROWS)


def _lat_na_kernel(q_ref, k_ref, v_ref, kc_ref, vc_ref, bias_ref, o_ref):
    start = pl.multiple_of(_na_window_start(pl.program_id(1)) * GRID_W, NA_QTOK)
    for h in range(NA_HEADS):
        sl = slice(h * NA_HEAD_DIM, (h + 1) * NA_HEAD_DIM)
        q = q_ref[:, sl]
        s_loc = _dot_nt(q, k_ref[pl.ds(start, NA_WIN_TOK), sl]) + bias_ref[0, h]
        s_ctx = _dot_nt(q, kc_ref[:, sl])
        o = _softmax_av([s_loc, s_ctx], [v_ref[pl.ds(start, NA_WIN_TOK), sl], vc_ref[:, sl]])
        o_ref[:, sl] = o.astype(BF)


def _na_bias_variant(step):
    return jnp.where(step == 0, 0, jnp.where(step == NA_STEPS - 1, 2, 1))


def _lat_na_attention(naq, nak, nav, kc, vc, bias):
    whole = pl.Buffered(1)
    return pl.pallas_call(
        _lat_na_kernel,
        out_shape=jax.ShapeDtypeStruct((N_TOK, NA_WIDTH), BF),
        grid=(DEC_BATCH, NA_STEPS),
        in_specs=[pl.BlockSpec((NA_QTOK, NA_WIDTH), lambda b, j: (b * NA_STEPS + j, 0)),
                  pl.BlockSpec((DEC_SEQ, NA_WIDTH), lambda b, j: (b, 0), pipeline_mode=whole),
                  pl.BlockSpec((DEC_SEQ, NA_WIDTH), lambda b, j: (b, 0), pipeline_mode=whole),
                  pl.BlockSpec((PAST_LEN, NA_WIDTH), lambda b, j: (b, 0)),
                  pl.BlockSpec((PAST_LEN, NA_WIDTH), lambda b, j: (b, 0)),
                  pl.BlockSpec((1, NA_HEADS, NA_QTOK, NA_WIN_TOK), lambda b, j: (_na_bias_variant(j), 0, 0, 0))],
        out_specs=pl.BlockSpec((NA_QTOK, NA_WIDTH), lambda b, j: (b * NA_STEPS + j, 0)),
        compiler_params=_cparams(("parallel", "arbitrary")),
        name="latent_neighbourhood_attention",
    )(naq, nak, nav, kc, vc, bias)


def _lat_mla_kernel(q_ref, k_ref, v_ref, kc_ref, vc_ref, *rest):
    n_cast = (len(rest) - 1) // 2
    o_ref = rest[n_cast]
    _cast_rows(rest[:n_cast], rest[n_cast + 1:])
    heads = range(MLA_HPS)
    qk = [slice(h * MLA_QK_PAD, (h + 1) * MLA_QK_PAD) for h in heads]
    vv = [slice(h * MLA_V_DIM, (h + 1) * MLA_V_DIM) for h in heads]
    q = [q_ref[:, qk[h]] for h in heads]
    m, l, acc = [], [], []
    for h in heads:
        s = _dot_nt(q[h], kc_ref[:, qk[h]])
        m.append(s.max(axis=-1, keepdims=True))
        p = jnp.exp2(s - m[h])
        l.append(p.sum(axis=-1, keepdims=True))
        acc.append(_dot(p.astype(BF), vc_ref[:, vv[h]]))
    for c in range(DEC_SEQ // MLA_KB):
        ks = slice(c * MLA_KB, (c + 1) * MLA_KB)
        for h in heads:
            s = _dot_nt(q[h], k_ref[ks, qk[h]])
            m_new = jnp.maximum(m[h], s.max(axis=-1, keepdims=True))
            a = jnp.exp2(m[h] - m_new)
            p = jnp.exp2(s - m_new)
            l[h] = a * l[h] + p.sum(axis=-1, keepdims=True)
            acc[h] = a * acc[h] + _dot(p.astype(BF), v_ref[ks, vv[h]])
            m[h] = m_new
    for h in heads:
        o_ref[:, vv[h]] = (acc[h] * (1.0 / l[h])).astype(BF)


def _lat_mla_attention(qm, km, vm, kmc, vmc, cast_weights=()):
    nq = DEC_SEQ // MLA_TQ
    nh = MLA_HEADS // MLA_HPS
    qw, vw = MLA_HPS * MLA_QK_PAD, MLA_HPS * MLA_V_DIM
    c_in, c_out, c_shape = _cast_specs(cast_weights, DEC_BATCH * nh * nq, lambda b, h, i: (b * nh + h) * nq + i)
    return pl.pallas_call(
        _lat_mla_kernel,
        out_shape=[jax.ShapeDtypeStruct((N_TOK, MLA_WIDTH), BF)] + c_shape,
        grid=(DEC_BATCH, nh, nq),
        in_specs=[pl.BlockSpec((MLA_TQ, qw), lambda b, h, i: (b * nq + i, h)),
                  pl.BlockSpec((DEC_SEQ, qw), lambda b, h, i: (b, h)),
                  pl.BlockSpec((DEC_SEQ, vw), lambda b, h, i: (b, h)),
                  pl.BlockSpec((PAST_LEN, qw), lambda b, h, i: (b, h)),
                  pl.BlockSpec((PAST_LEN, vw), lambda b, h, i: (b, h))] + c_in,
        out_specs=[pl.BlockSpec((MLA_TQ, vw), lambda b, h, i: (b * nq + i, h))] + c_out,
        compiler_params=_cparams(("arbitrary", "arbitrary", "arbitrary")),
        name="latent_mla_attention",
    )(qm, km, vm, kmc, vmc, *cast_weights)


def _oproj_kernel(ana_ref, amla_ref, x_ref, mod_ref, w1_ref, w2_ref, g_ref, b_ref, *rest):
    n_cast = (len(rest) - 1) // 2
    o_ref = rest[n_cast]
    _cast_rows(rest[:n_cast], rest[n_cast + 1:])
    o = _dot(ana_ref[...], w1_ref[...]) + _dot(amla_ref[...], w2_ref[...])
    y = ALPHA * x_ref[...] + mod_ref[0, 2:3, :] * o
    o_ref[...] = _layernorm(y, g_ref[...], b_ref[...])


def _out_projection(ana, amla, x2d, mod3, mod_map1, w_o, g, b, cast_weights=()):
    assert NA_WIDTH == MLA_WIDTH
    tm = OPROJ_TM
    steps = N_TOK // tm
    row = lambda i: (i, 0)
    const = lambda i: (0, 0)
    whole = pl.Buffered(1)
    c_in, c_out, c_shape = _cast_specs(cast_weights, steps)
    return pl.pallas_call(
        _oproj_kernel,
        out_shape=[jax.ShapeDtypeStruct((N_TOK, D_MODEL), F32)] + c_shape,
        grid=(steps,),
        in_specs=[pl.BlockSpec((tm, NA_WIDTH), row), pl.BlockSpec((tm, MLA_WIDTH), row),
                  pl.BlockSpec((tm, D_MODEL), row),
                  pl.BlockSpec((1, 6, D_MODEL), lambda i: (mod_map1(i, tm), 0, 0)),
                  pl.BlockSpec((NA_WIDTH, D_MODEL), const, pipeline_mode=whole),
                  pl.BlockSpec((MLA_WIDTH, D_MODEL), lambda i: (1, 0), pipeline_mode=whole),
                  pl.BlockSpec((1, D_MODEL), const), pl.BlockSpec((1, D_MODEL), const)] + c_in,
        out_specs=[pl.BlockSpec((tm, D_MODEL), row)] + c_out,
        compiler_params=_cparams(("parallel",)),
        name="out_projection_ln1",
    )(ana, amla, x2d, mod3, w_o, w_o, g, b, *cast_weights)


def _ffn_kernel(x_ref, mod_ref, wg_ref, wu_ref, wd_ref, g_ref, b_ref, o_ref, h_scr):
    j = pl.program_id(1)

    @pl.when(j == 0)
    def _():
        h_scr[...] = (x_ref[...] * (1.0 + mod_ref[0, 4:5, :]) + mod_ref[0, 3:4, :]).astype(BF)
        o_ref[...] = jnp.zeros_like(o_ref)

    hb = h_scr[...]
    gate = _dot(hb, wg_ref[...])
    up = _dot(hb, wu_ref[...])
    act = (gate * _sigmoid(gate) * up).astype(BF)
    for n in range(D_MODEL // FFN_TN):
        sl = slice(n * FFN_TN, (n + 1) * FFN_TN)
        o_ref[:, sl] += _dot(act, wd_ref[:, sl])

    @pl.when(j == pl.num_programs(1) - 1)
    def _():
        y = ALPHA * x_ref[...] + mod_ref[0, 5:6, :] * o_ref[...]
        o_ref[...] = _layernorm(y, g_ref[...], b_ref[...])


def _ffn(x1, mod3, mod_map1, w_gu, w_down, g, b):
    tm, tf = FFN_TM, FFN_TF
    nf = D_FF // tf
    const = lambda i, j: (0, 0)
    row = lambda i, j: (i, 0)
    return pl.pallas_call(
        _ffn_kernel,
        out_shape=jax.ShapeDtypeStruct((N_TOK, D_MODEL), F32),
        grid=(N_TOK // tm, nf),
        in_specs=[pl.BlockSpec((tm, D_MODEL), row),
                  pl.BlockSpec((1, 6, D_MODEL), lambda i, j: (mod_map1(i, tm), 0, 0)),
                  pl.BlockSpec((D_MODEL, tf), lambda i, j: (0, j)),
                  pl.BlockSpec((D_MODEL, tf), lambda i, j: (0, j + nf)),
                  pl.BlockSpec((tf, D_MODEL), lambda i, j: (j, 0)),
                  pl.BlockSpec((1, D_MODEL), const), pl.BlockSpec((1, D_MODEL), const)],
        out_specs=pl.BlockSpec((tm, D_MODEL), row),
        scratch_shapes=[pltpu.VMEM((tm, D_MODEL), BF)],
        compiler_params=_cparams(("parallel", "arbitrary"), FFN_VMEM_LIMIT),
        name="swiglu_ln2",
    )(x1, mod3, w_gu, w_gu, w_down, g, b)


def _prep_weights(w_main_b, w_rope_b, q_a_norm, kv_a_norm, w_q_b, w_kv_b):
    wq = w_q_b.astype(BF).reshape(Q_LORA, MLA_HEADS, MLA_NOPE_DIM + MLA_ROPE_DIM)
    wkv = w_kv_b.astype(BF).reshape(KV_LORA, MLA_HEADS, MLA_NOPE_DIM + MLA_V_DIM)
    pad = LANES - MLA_ROPE_DIM
    return {
        "w_main": w_main_b,
        "w_rope": w_rope_b,
        "q_g": q_a_norm.reshape(1, Q_LORA),
        "kv_g": kv_a_norm.reshape(1, KV_LORA),
        "w_qn": wq[:, :, :MLA_NOPE_DIM].reshape(Q_LORA, NA_WIDTH),
        "w_qr": jnp.pad(wq[:, :, MLA_NOPE_DIM:], ((0, 0), (0, 0), (0, pad))).reshape(Q_LORA, NA_WIDTH),
        "w_kn": wkv[:, :, :MLA_NOPE_DIM].reshape(KV_LORA, NA_WIDTH),
        "w_v": wkv[:, :, MLA_NOPE_DIM:].reshape(KV_LORA, MLA_WIDTH),
    }


def _rope_tables():
    half = MLA_ROPE_DIM // 2
    inv_freq = np.float32(ROPE_THETA) ** (-np.arange(0, half, 2, dtype=np.float32) / np.float32(half))
    t = np.arange(DEC_SEQ, dtype=np.int32)

    def tables(pos):
        ang = pos.astype(np.float32)[:, None] * inv_freq
        s = np.sin(ang)
        return np.concatenate([np.cos(ang)] * 2, -1), np.concatenate([-s, s], -1)

    cr, sr = tables(t // GRID_W)
    cc, sc = tables(t % GRID_W)
    pad = LANES - MLA_ROPE_DIM
    cos = np.concatenate([cr, cc, np.ones((DEC_SEQ, pad), np.float32)], -1)
    sin = np.concatenate([sr, sc, np.zeros((DEC_SEQ, pad), np.float32)], -1)
    return jnp.asarray(cos, F32), jnp.asarray(sin, F32)


def _na_bias_plan():
    plan = []
    for step in NA_VARIANT_STEPS:
        start = min(max(NA_QROWS * step - NA_WIN_ROWS // 2, 0), GRID_H - NA_KROWS)
        per_row = []
        for i in range(NA_QROWS):
            r = NA_QROWS * step + i
            rs = min(max(r - NA_WIN_ROWS // 2, 0), GRID_H - NA_WIN_ROWS)
            per_row.append([(start + u - r + NA_WIN_ROWS - 1, rs <= start + u < rs + NA_WIN_ROWS)
                            for u in range(NA_KROWS)])
        plan.append(per_row)
    return plan


def _na_bias_kernel(ra_ref, rb_ref, o_ref):
    shape = (GRID_W, LANES)
    lane = lax.broadcasted_iota(jnp.int32, shape, 1)
    qc = lax.broadcasted_iota(jnp.int32, shape, 0)
    kc = lane % GRID_W
    cs = jnp.clip(qc - NA_WIN_COLS // 2, 0, GRID_W - NA_WIN_COLS)
    ok_col = (kc >= cs) & (kc < cs + NA_WIN_COLS)
    left = lane < GRID_W
    neg = jnp.full(shape, NEG, F32)
    for v, per_row in enumerate(_na_bias_plan()):
        for i, rows in enumerate(per_row):
            for t in range(len(rows) // 2):
                (dra, oka), (drb, okb) = rows[2 * t], rows[2 * t + 1]
                if oka and okb:
                    ok = ok_col
                elif oka:
                    ok = ok_col & left
                elif okb:
                    ok = ok_col & jnp.logical_not(left)
                for h in range(NA_HEADS):
                    if oka or okb:
                        src = jnp.zeros((1, LANES), F32)
                        if oka:
                            src = src + ra_ref[h, dra:dra + 1, :]
                        if okb:
                            src = src + rb_ref[h, drb:drb + 1, :]
                        val = pltpu.roll(jnp.broadcast_to(src, shape), LANES - (NA_WIN_COLS - 1), 1,
                                         stride=1, stride_axis=0)
                        tile = jnp.where(ok, val * LOG2E, neg)
                    else:
                        tile = neg
                    o_ref[v, h, i * GRID_W:(i + 1) * GRID_W, t * LANES:(t + 1) * LANES] = tile


def _na_bias_tables(rpb):
    n_dr, n_dc = 2 * NA_WIN_ROWS - 1, 2 * NA_WIN_COLS - 1
    ra = jnp.pad(rpb, ((0, 0), (0, 16 - n_dr), (0, LANES - n_dc)))
    rb = jnp.pad(rpb, ((0, 0), (0, 16 - n_dr), (GRID_W, LANES - GRID_W - n_dc)))
    return pl.pallas_call(
        _na_bias_kernel,
        out_shape=jax.ShapeDtypeStruct((len(NA_VARIANT_STEPS), NA_HEADS, NA_QTOK, NA_WIN_TOK), F32),
        compiler_params=pltpu.CompilerParams(vmem_limit_bytes=VMEM_LIMIT),
        name="na_bias_tables",
    )(ra, rb)


def kernel(x_prompt, x_sample, cache_na_k, cache_na_v, cache_mla_ckv, cache_mla_krope, c, c_ctx,
           w_mod, b_mod, w_in, q_a_norm, kv_a_norm, w_q_b, w_kv_b, na_rpb, w_o,
           ln1_g, ln1_b, w_gu, w_down, ln2_g, ln2_b):
    cond8 = jnp.concatenate([c_ctx[None], c, jnp.zeros((8 - 1 - DEC_BATCH, D_MODEL), F32)], 0)
    mod, w_main_b, w_rope_b = _modulation(cond8, w_mod[0], b_mod[0][None], w_in[0].T)
    mod3 = mod.reshape(8, 6, D_MODEL)

    w = _prep_weights(w_main_b, w_rope_b, q_a_norm[0], kv_a_norm[0], w_q_b[0], w_kv_b[0])
    g1, b1 = ln1_g[0][None], ln1_b[0][None]
    g2, b2 = ln2_g[0][None], ln2_b[0][None]
    cos, sin = _rope_tables()

    xp = x_prompt.reshape(N_TOK, D_MODEL)
    xs = x_sample.reshape(N_TOK, D_MODEL)
    ctx_mod = lambda i, tm: 0
    lat_mod = lambda i, tm: 1 + i // (DEC_SEQ // tm)

    naq, nak, nav, lat, kr, st_k, st_v = _projection(xp, mod3, ctx_mod, w, state=True)
    qm, km, vm, st_ckv = _expand(lat, kr, w, rope=False, state=True)
    st_kr = kr[:, :MLA_ROPE_DIM]
    a_na, a_mla, w_o_b = _ctx_attention(naq, nak, nav, qm, km, vm, (w_o[0],))
    x1_ctx, w_down_b = _out_projection(a_na, a_mla, xp, mod3, ctx_mod, w_o_b, g1, b1, (w_down[0],))

    naq, nak, nav, lat, kr = _projection(xs, mod3, lat_mod, w, state=False)
    qm, km, vm = _expand(lat, kr, w, rope=True, state=False, cos=cos, sin=sin)
    kr_cache = jnp.pad(cache_mla_krope[:, 0].reshape(DEC_BATCH * PAST_LEN, MLA_ROPE_DIM),
                       ((0, 0), (0, LANES - MLA_ROPE_DIM)))
    kmc, vmc = _cache_kv(cache_mla_ckv[:, 0].reshape(DEC_BATCH * PAST_LEN, KV_LORA), kr_cache, w)
    kc = cache_na_k[:, 0].reshape(DEC_BATCH * PAST_LEN, NA_WIDTH).astype(BF)
    vc = cache_na_v[:, 0].reshape(DEC_BATCH * PAST_LEN, NA_WIDTH).astype(BF)
    a_na = _lat_na_attention(naq, nak, nav, kc, vc, _na_bias_tables(na_rpb[0]))
    a_mla, w_gu_b = _lat_mla_attention(qm, km, vm, kmc, vmc, (w_gu[0],))
    x1_lat, = _out_projection(a_na, a_mla, xs, mod3, lat_mod, w_o_b, g1, b1)
    yp = _ffn(x1_ctx, mod3, ctx_mod, w_gu_b, w_down_b, g2, b2)
    ys = _ffn(x1_lat, mod3, lat_mod, w_gu_b, w_down_b, g2, b2)

    return (yp.reshape(BATCH, SEQ, D_MODEL),
            ys.reshape(DEC_BATCH, DEC_SEQ, D_MODEL),
            st_k.reshape(BATCH, 1, SEQ, NA_HEADS, NA_HEAD_DIM),
            st_v.reshape(BATCH, 1, SEQ, NA_HEADS, NA_HEAD_DIM),
            st_ckv.reshape(BATCH, 1, SEQ, KV_LORA),
            st_kr.reshape(BATCH, 1, SEQ, MLA_ROPE_DIM))
```

```python
import functools

import numpy as np
import jax
import jax.numpy as jnp
from jax import lax
from jax.experimental import pallas as pl
from jax.experimental.pallas import tpu as pltpu

D_MODEL = 2048
BATCH = 32
SEQ = 256
DEC_BATCH = 2
DEC_SEQ = 4096
PAST_LEN = 512
GRID_W = 64
GRID_H = DEC_SEQ // GRID_W
NA_HEADS = 8
NA_HEAD_DIM = 128
NA_WIN_ROWS = 8
NA_WIN_COLS = 16
MLA_HEADS = 8
MLA_NOPE_DIM = 128
MLA_ROPE_DIM = 64
MLA_V_DIM = 128
Q_LORA = 512
KV_LORA = 512
NA_WIDTH = NA_HEADS * NA_HEAD_DIM
MLA_WIDTH = MLA_HEADS * MLA_V_DIM
D_FF = -(-8 * D_MODEL // (3 * 256)) * 256
ROPE_THETA = 10000.0
LN_EPS = 1e-5
RMS_EPS = 1e-6
ALPHA = 2.0 ** 0.25
LOG2E = 1.4426950408889634
NA_SCALE = NA_HEAD_DIM ** -0.5 * LOG2E
MLA_SCALE = (MLA_NOPE_DIM + MLA_ROPE_DIM) ** -0.5 * LOG2E

BF = jnp.bfloat16
F32 = jnp.float32
LANES = 128
MLA_QK_PAD = 2 * LANES
NEG = -1e30
VMEM_LIMIT = 60 * 1024 * 1024
FFN_VMEM_LIMIT = 63 * 1024 * 1024

N_TOK = BATCH * SEQ
assert N_TOK == DEC_BATCH * DEC_SEQ

MOD_TN = 768
PROJ_TM = 512
CTX_BPS = 2
OPROJ_TM = 512
FFN_TM = 1024
FFN_TF = 512
FFN_TN = 512
MLA_TQ = 1024
MLA_KB = 512
MLA_HPS = 2
NA_QROWS = 4
NA_STEPS = GRID_H // NA_QROWS
NA_KROWS = NA_QROWS + NA_WIN_ROWS
NA_QTOK = NA_QROWS * GRID_W
NA_WIN_TOK = NA_KROWS * GRID_W
NA_VARIANT_STEPS = (0, 1, NA_STEPS - 1)
assert NA_QROWS == NA_WIN_ROWS // 2 and NA_KROWS % 2 == 0


def _cparams(sem, vmem_limit=VMEM_LIMIT):
    return pltpu.CompilerParams(dimension_semantics=sem, vmem_limit_bytes=vmem_limit)


def _dot(a, b):
    return jnp.dot(a, b, preferred_element_type=F32)


def _dot_nt(a, b):
    return lax.dot_general(a, b, (((1,), (1,)), ((), ())), preferred_element_type=F32)


def _sigmoid(x):
    return 1.0 / (1.0 + jnp.exp(-x))


def _layernorm(y, g, b):
    mu = jnp.mean(y, axis=-1, keepdims=True)
    yc = y - mu
    var = jnp.mean(yc * yc, axis=-1, keepdims=True)
    return yc * lax.rsqrt(var + LN_EPS) * g + b


def _rmsnorm(x, g):
    return x * lax.rsqrt(jnp.mean(x * x, axis=-1, keepdims=True) + RMS_EPS) * g


def _mod_kernel(c_ref, w_ref, b_ref, wt_ref, wtr_ref, o_ref, wb_ref, wrb_ref):
    wb_ref[...] = wt_ref[...].T.astype(BF)
    rope_cols = wtr_ref[...].T.astype(BF)
    wrb_ref[...] = jnp.concatenate([rope_cols, jnp.zeros_like(rope_cols)], axis=1)
    c = c_ref[...]
    s = (c * _sigmoid(c)).astype(BF)
    o_ref[...] = _dot(s, w_ref[...].astype(BF)) + b_ref[...]


def _modulation(cond8, w_mod, b_mod, w_in_t):
    tn = MOD_TN
    n = 6 * D_MODEL
    steps = n // tn
    main = 4 * NA_WIDTH
    cols = main // steps
    return pl.pallas_call(
        _mod_kernel,
        out_shape=[jax.ShapeDtypeStruct((8, n), F32), jax.ShapeDtypeStruct((D_MODEL, main), BF),
                   jax.ShapeDtypeStruct((D_MODEL, LANES), BF)],
        grid=(steps,),
        in_specs=[pl.BlockSpec((8, D_MODEL), lambda j: (0, 0)),
                  pl.BlockSpec((D_MODEL, tn), lambda j: (0, j)),
                  pl.BlockSpec((1, tn), lambda j: (0, j)),
                  pl.BlockSpec((cols, D_MODEL), lambda j: (j, 0)),
                  pl.BlockSpec((MLA_ROPE_DIM, D_MODEL), lambda j: (main // MLA_ROPE_DIM, 0))],
        out_specs=[pl.BlockSpec((8, tn), lambda j: (0, j)), pl.BlockSpec((D_MODEL, cols), lambda j: (0, j)),
                   pl.BlockSpec((D_MODEL, LANES), lambda j: (0, 0))],
        compiler_params=_cparams(("arbitrary",)),
        name="modulation",
    )(cond8, w_mod, b_mod, w_in_t, w_in_t)


def _rope128(x, cos, sin):
    lane = lax.broadcasted_iota(jnp.int32, x.shape, 1)
    partner = jnp.where(lane % 32 < 16, pltpu.roll(x, LANES - 16, 1), pltpu.roll(x, 16, 1))
    return x * cos + partner * sin


def _expand_into(lat, kr, w_refs, cos_sin, qm_ref, km_ref, vm_ref, ckv_ref):
    qg_ref, kvg_ref, wqn_ref, wqr_ref, wkn_ref, wv_ref = w_refs
    qn = _rmsnorm(lat[:, :Q_LORA], qg_ref[...]).astype(BF)
    qnope = _dot(qn, wqn_ref[...]) * MLA_SCALE
    qrope = _dot(qn, wqr_ref[...]) * MLA_SCALE
    for h in range(MLA_HEADS):
        sl = slice(h * LANES, (h + 1) * LANES)
        r = qrope[:, sl]
        if cos_sin is not None:
            r = _rope128(r, cos_sin[0][...], cos_sin[1][...])
        qm_ref[:, h * MLA_QK_PAD: h * MLA_QK_PAD + LANES] = qnope[:, sl].astype(BF)
        qm_ref[:, h * MLA_QK_PAD + LANES: (h + 1) * MLA_QK_PAD] = r.astype(BF)

    ckv = _rmsnorm(lat[:, Q_LORA:], kvg_ref[...])
    if ckv_ref is not None:
        ckv_ref[...] = ckv
    cb = ckv.astype(BF)
    kn = _dot(cb, wkn_ref[...])
    vm_ref[...] = _dot(cb, wv_ref[...]).astype(BF)
    if cos_sin is not None:
        kr = _rope128(kr, cos_sin[0][...], cos_sin[1][...])
    krb = kr.astype(BF)
    for h in range(MLA_HEADS):
        km_ref[:, h * MLA_QK_PAD: h * MLA_QK_PAD + LANES] = kn[:, h * LANES:(h + 1) * LANES].astype(BF)
        km_ref[:, h * MLA_QK_PAD + LANES: (h + 1) * MLA_QK_PAD] = krb


def _proj_kernel(x_ref, mod_ref, wm_ref, wr_ref, *rest, fused):
    if fused:
        w_refs, cos_sin = rest[:6], rest[6:8]
        naq_ref, nak_ref, nav_ref, qm_ref, km_ref, vm_ref = rest[8:]
        state_refs = ()
    else:
        naq_ref, nak_ref, nav_ref, lat_ref, kr_ref = rest[:5]
        state_refs = rest[5:]
    hb = (x_ref[...] * (1.0 + mod_ref[0, 1:2, :]) + mod_ref[0, 0:1, :]).astype(BF)
    for n, o_ref in enumerate((naq_ref, nak_ref, nav_ref)):
        z = _dot(hb, wm_ref[:, n * NA_WIDTH:(n + 1) * NA_WIDTH])
        if state_refs and n > 0:
            state_refs[n - 1][...] = z.reshape(z.shape[0], NA_HEADS, NA_HEAD_DIM)
        if n == 0:
            z = z * NA_SCALE
        o_ref[...] = z.astype(BF)
    lat = _dot(hb, wm_ref[:, 3 * NA_WIDTH:])
    kr = _dot(hb, wr_ref[...])
    if fused:
        _expand_into(lat, kr, w_refs, cos_sin, qm_ref, km_ref, vm_ref, None)
    else:
        lat_ref[...] = lat
        kr_ref[...] = kr


def _expand_weight_specs(const):
    single = pl.Buffered(1)
    return [pl.BlockSpec((1, Q_LORA), const), pl.BlockSpec((1, KV_LORA), const),
            pl.BlockSpec((Q_LORA, NA_WIDTH), const, pipeline_mode=single),
            pl.BlockSpec((Q_LORA, NA_WIDTH), const, pipeline_mode=single),
            pl.BlockSpec((KV_LORA, NA_WIDTH), const, pipeline_mode=single),
            pl.BlockSpec((KV_LORA, MLA_WIDTH), const, pipeline_mode=single)]


def _expand_weights(w):
    return [w["q_g"], w["kv_g"], w["w_qn"], w["w_qr"], w["w_kn"], w["w_v"]]


def _projection(x2d, mod3, mod_map1, w, *, fused, cos=None, sin=None):
    tm = PROJ_TM
    tiles_per_batch = DEC_SEQ // tm
    const = lambda i: (0, 0)
    row = lambda i: (i, 0)
    whole = pl.Buffered(1)
    wide = MLA_HEADS * MLA_QK_PAD
    in_specs = [pl.BlockSpec((tm, D_MODEL), row),
                pl.BlockSpec((1, 6, D_MODEL), lambda i: (mod_map1(i, tm), 0, 0)),
                pl.BlockSpec((D_MODEL, 4 * NA_WIDTH), const, pipeline_mode=whole),
                pl.BlockSpec((D_MODEL, LANES), const, pipeline_mode=whole)]
    args = [x2d, mod3, w["w_main"], w["w_rope"]]
    out_shape = [jax.ShapeDtypeStruct((N_TOK, NA_WIDTH), BF)] * 3
    out_specs = [pl.BlockSpec((tm, NA_WIDTH), row)] * 3
    if fused:
        in_specs += _expand_weight_specs(const) + [pl.BlockSpec((tm, LANES), lambda i: (i % tiles_per_batch, 0))] * 2
        args += _expand_weights(w) + [cos, sin]
        out_shape += [jax.ShapeDtypeStruct((N_TOK, wide), BF)] * 2 + [jax.ShapeDtypeStruct((N_TOK, MLA_WIDTH), BF)]
        out_specs += [pl.BlockSpec((tm, wide), row)] * 2 + [pl.BlockSpec((tm, MLA_WIDTH), row)]
    else:
        out_shape += [jax.ShapeDtypeStruct((N_TOK, Q_LORA + KV_LORA), F32), jax.ShapeDtypeStruct((N_TOK, LANES), F32)]
        out_specs += [pl.BlockSpec((tm, Q_LORA + KV_LORA), row), pl.BlockSpec((tm, LANES), row)]
        out_shape += [jax.ShapeDtypeStruct((N_TOK, NA_HEADS, NA_HEAD_DIM), F32)] * 2
        out_specs += [pl.BlockSpec((tm, NA_HEADS, NA_HEAD_DIM), lambda i: (i, 0, 0))] * 2
    return pl.pallas_call(
        functools.partial(_proj_kernel, fused=fused),
        out_shape=out_shape,
        grid=(N_TOK // tm,),
        in_specs=in_specs,
        out_specs=out_specs,
        compiler_params=_cparams(("parallel",)),
        name="input_projection_latent" if fused else "input_projection_context",
    )(*args)


def _cache_kv_kernel(ckv_ref, kr_ref, wkn_ref, wv_ref, km_ref, vm_ref):
    cb = ckv_ref[...].astype(BF)
    kn = _dot(cb, wkn_ref[...])
    vm_ref[...] = _dot(cb, wv_ref[...]).astype(BF)
    krb = kr_ref[...].astype(BF)
    for h in range(MLA_HEADS):
        km_ref[:, h * MLA_QK_PAD: h * MLA_QK_PAD + LANES] = kn[:, h * LANES:(h + 1) * LANES].astype(BF)
        km_ref[:, h * MLA_QK_PAD + LANES: (h + 1) * MLA_QK_PAD] = krb


def _cache_kv(ckv2d, kr2d, w):
    n = DEC_BATCH * PAST_LEN
    tm = PAST_LEN
    const = lambda i: (0, 0)
    row = lambda i: (i, 0)
    return pl.pallas_call(
        _cache_kv_kernel,
        out_shape=[jax.ShapeDtypeStruct((n, MLA_HEADS * MLA_QK_PAD), BF),
                   jax.ShapeDtypeStruct((n, MLA_WIDTH), BF)],
        grid=(n // tm,),
        in_specs=[pl.BlockSpec((tm, KV_LORA), row), pl.BlockSpec((tm, LANES), row),
                  pl.BlockSpec((KV_LORA, NA_WIDTH), const), pl.BlockSpec((KV_LORA, MLA_WIDTH), const)],
        out_specs=[pl.BlockSpec((tm, MLA_HEADS * MLA_QK_PAD), row), pl.BlockSpec((tm, MLA_WIDTH), row)],
        compiler_params=_cparams(("parallel",)),
        name="cache_kv_expand",
    )(ckv2d, kr2d, w["w_kn"], w["w_v"])


def _softmax_av(s_list, v_list):
    m = functools.reduce(jnp.maximum, [s.max(axis=-1, keepdims=True) for s in s_list])
    p_list = [jnp.exp2(s - m) for s in s_list]
    l = functools.reduce(jnp.add, [p.sum(axis=-1, keepdims=True) for p in p_list])
    o = functools.reduce(jnp.add, [_dot(p.astype(BF), v) for p, v in zip(p_list, v_list)])
    return o * (1.0 / l)


def _cast_rows(src_refs, dst_refs):
    for src, dst in zip(src_refs, dst_refs):
        dst[...] = src[...].astype(BF)


def _cast_specs(weights, steps, linear_step=lambda i: i):
    in_specs, out_specs, out_shape = [], [], []
    for wgt in weights:
        rows, cols = wgt.shape
        assert rows % (16 * steps) == 0
        blk = pl.BlockSpec((rows // steps, cols), lambda *g: (linear_step(*g), 0))
        in_specs.append(blk)
        out_specs.append(blk)
        out_shape.append(jax.ShapeDtypeStruct((rows, cols), BF))
    return in_specs, out_specs, out_shape


def _ctx_attn_kernel(q_ref, k_ref, v_ref, lat_ref, kr_ref, *rest):
    w_refs, rest = rest[:6], rest[6:]
    n_cast = (len(rest) - 6) // 2
    ona_ref, omla_ref, ckv_ref = rest[n_cast:n_cast + 3]
    qm_ref, km_ref, vm_ref = rest[2 * n_cast + 3:]
    _cast_rows(rest[:n_cast], rest[n_cast + 3:2 * n_cast + 3])
    _expand_into(lat_ref[...], kr_ref[...], w_refs, None, qm_ref, km_ref, vm_ref, ckv_ref)
    for b in range(CTX_BPS):
        rows = slice(b * SEQ, (b + 1) * SEQ)
        for h in range(NA_HEADS):
            sl = slice(h * NA_HEAD_DIM, (h + 1) * NA_HEAD_DIM)
            s = _dot_nt(q_ref[rows, sl], k_ref[rows, sl])
            ona_ref[rows, sl] = _softmax_av([s], [v_ref[rows, sl]]).astype(BF)
        for h in range(MLA_HEADS):
            sq = slice(h * MLA_QK_PAD, (h + 1) * MLA_QK_PAD)
            sv = slice(h * MLA_V_DIM, (h + 1) * MLA_V_DIM)
            s = _dot_nt(qm_ref[rows, sq], km_ref[rows, sq])
            omla_ref[rows, sv] = _softmax_av([s], [vm_ref[rows, sv]]).astype(BF)


def _ctx_attention(naq, nak, nav, lat, kr, w, cast_weights):
    row = lambda b: (b, 0)
    const = lambda b: (0, 0)
    wide = MLA_HEADS * MLA_QK_PAD
    tb = CTX_BPS * SEQ
    steps = BATCH // CTX_BPS
    c_in, c_out, c_shape = _cast_specs(cast_weights, steps)
    return pl.pallas_call(
        _ctx_attn_kernel,
        out_shape=[jax.ShapeDtypeStruct((N_TOK, NA_WIDTH), BF), jax.ShapeDtypeStruct((N_TOK, MLA_WIDTH), BF),
                   jax.ShapeDtypeStruct((N_TOK, KV_LORA), F32)] + c_shape,
        grid=(steps,),
        in_specs=[pl.BlockSpec((tb, NA_WIDTH), row)] * 3
                 + [pl.BlockSpec((tb, Q_LORA + KV_LORA), row), pl.BlockSpec((tb, LANES), row)]
                 + _expand_weight_specs(const) + c_in,
        out_specs=[pl.BlockSpec((tb, NA_WIDTH), row), pl.BlockSpec((tb, MLA_WIDTH), row),
                   pl.BlockSpec((tb, KV_LORA), row)] + c_out,
        scratch_shapes=[pltpu.VMEM((tb, wide), BF), pltpu.VMEM((tb, wide), BF), pltpu.VMEM((tb, MLA_WIDTH), BF)],
        compiler_params=_cparams(("parallel",)),
        name="context_attention",
    )(naq, nak, nav, lat, kr, *_expand_weights(w), *cast_weights)


def _na_window_start(step):
    return jnp.clip(NA_QROWS * step - NA_WIN_ROWS // 2, 0, GRID_H - NA_KROWS)


def _lat_na_kernel(q_ref, k_ref, v_ref, kc_ref, vc_ref, bias_ref, o_ref):
    start = pl.multiple_of(_na_window_start(pl.program_id(1)) * GRID_W, NA_QTOK)
    for h in range(NA_HEADS):
        sl = slice(h * NA_HEAD_DIM, (h + 1) * NA_HEAD_DIM)
        q = q_ref[:, sl]
        s_loc = _dot_nt(q, k_ref[pl.ds(start, NA_WIN_TOK), sl]) + bias_ref[0, h]
        s_ctx = _dot_nt(q, kc_ref[:, sl])
        o = _softmax_av([s_loc, s_ctx], [v_ref[pl.ds(start, NA_WIN_TOK), sl], vc_ref[:, sl]])
        o_ref[:, sl] = o.astype(BF)


def _na_bias_variant(step):
    return jnp.where(step == 0, 0, jnp.where(step == NA_STEPS - 1, 2, 1))


def _lat_na_attention(naq, nak, nav, kc, vc, bias):
    whole = pl.Buffered(1)
    return pl.pallas_call(
        _lat_na_kernel,
        out_shape=jax.ShapeDtypeStruct((N_TOK, NA_WIDTH), BF),
        grid=(DEC_BATCH, NA_STEPS),
        in_specs=[pl.BlockSpec((NA_QTOK, NA_WIDTH), lambda b, j: (b * NA_STEPS + j, 0)),
                  pl.BlockSpec((DEC_SEQ, NA_WIDTH), lambda b, j: (b, 0), pipeline_mode=whole),
                  pl.BlockSpec((DEC_SEQ, NA_WIDTH), lambda b, j: (b, 0), pipeline_mode=whole),
                  pl.BlockSpec((PAST_LEN, NA_WIDTH), lambda b, j: (b, 0)),
                  pl.BlockSpec((PAST_LEN, NA_WIDTH), lambda b, j: (b, 0)),
                  pl.BlockSpec((1, NA_HEADS, NA_QTOK, NA_WIN_TOK), lambda b, j: (_na_bias_variant(j), 0, 0, 0))],
        out_specs=pl.BlockSpec((NA_QTOK, NA_WIDTH), lambda b, j: (b * NA_STEPS + j, 0)),
        compiler_params=_cparams(("parallel", "arbitrary")),
        name="latent_neighbourhood_attention",
    )(naq, nak, nav, kc, vc, bias)


def _lat_mla_kernel(q_ref, k_ref, v_ref, kc_ref, vc_ref, *rest):
    n_cast = (len(rest) - 1) // 2
    o_ref = rest[n_cast]
    _cast_rows(rest[:n_cast], rest[n_cast + 1:])
    heads = range(MLA_HPS)
    qk = [slice(h * MLA_QK_PAD, (h + 1) * MLA_QK_PAD) for h in heads]
    vv = [slice(h * MLA_V_DIM, (h + 1) * MLA_V_DIM) for h in heads]
    q = [q_ref[:, qk[h]] for h in heads]
    m, l, acc = [], [], []
    for h in heads:
        s = _dot_nt(q[h], kc_ref[:, qk[h]])
        m.append(s.max(axis=-1, keepdims=True))
        p = jnp.exp2(s - m[h])
        l.append(p.sum(axis=-1, keepdims=True))
        acc.append(_dot(p.astype(BF), vc_ref[:, vv[h]]))
    for c in range(DEC_SEQ // MLA_KB):
        ks = slice(c * MLA_KB, (c + 1) * MLA_KB)
        for h in heads:
            s = _dot_nt(q[h], k_ref[ks, qk[h]])
            m_new = jnp.maximum(m[h], s.max(axis=-1, keepdims=True))
            a = jnp.exp2(m[h] - m_new)
            p = jnp.exp2(s - m_new)
            l[h] = a * l[h] + p.sum(axis=-1, keepdims=True)
            acc[h] = a * acc[h] + _dot(p.astype(BF), v_ref[ks, vv[h]])
            m[h] = m_new
    for h in heads:
        o_ref[:, vv[h]] = (acc[h] * (1.0 / l[h])).astype(BF)


def _lat_mla_attention(qm, km, vm, kmc, vmc, cast_weights=()):
    nq = DEC_SEQ // MLA_TQ
    nh = MLA_HEADS // MLA_HPS
    qw, vw = MLA_HPS * MLA_QK_PAD, MLA_HPS * MLA_V_DIM
    c_in, c_out, c_shape = _cast_specs(cast_weights, DEC_BATCH * nh * nq, lambda b, h, i: (b * nh + h) * nq + i)
    return pl.pallas_call(
        _lat_mla_kernel,
        out_shape=[jax.ShapeDtypeStruct((N_TOK, MLA_WIDTH), BF)] + c_shape,
        grid=(DEC_BATCH, nh, nq),
        in_specs=[pl.BlockSpec((MLA_TQ, qw), lambda b, h, i: (b * nq + i, h)),
                  pl.BlockSpec((DEC_SEQ, qw), lambda b, h, i: (b, h)),
                  pl.BlockSpec((DEC_SEQ, vw), lambda b, h, i: (b, h)),
                  pl.BlockSpec((PAST_LEN, qw), lambda b, h, i: (b, h)),
                  pl.BlockSpec((PAST_LEN, vw), lambda b, h, i: (b, h))] + c_in,
        out_specs=[pl.BlockSpec((MLA_TQ, vw), lambda b, h, i: (b * nq + i, h))] + c_out,
        compiler_params=_cparams(("arbitrary", "arbitrary", "arbitrary")),
        name="latent_mla_attention",
    )(qm, km, vm, kmc, vmc, *cast_weights)


def _oproj_kernel(ana_ref, amla_ref, x_ref, mod_ref, w1_ref, w2_ref, g_ref, b_ref, *rest):
    n_cast = (len(rest) - 1) // 2
    o_ref = rest[n_cast]
    _cast_rows(rest[:n_cast], rest[n_cast + 1:])
    o = _dot(ana_ref[...], w1_ref[...]) + _dot(amla_ref[...], w2_ref[...])
    y = ALPHA * x_ref[...] + mod_ref[0, 2:3, :] * o
    o_ref[...] = _layernorm(y, g_ref[...], b_ref[...])


def _out_projection(ana, amla, x2d, mod3, mod_map1, w_o, g, b, cast_weights=()):
    assert NA_WIDTH == MLA_WIDTH
    tm = OPROJ_TM
    steps = N_TOK // tm
    row = lambda i: (i, 0)
    const = lambda i: (0, 0)
    whole = pl.Buffered(1)
    c_in, c_out, c_shape = _cast_specs(cast_weights, steps)
    return pl.pallas_call(
        _oproj_kernel,
        out_shape=[jax.ShapeDtypeStruct((N_TOK, D_MODEL), F32)] + c_shape,
        grid=(steps,),
        in_specs=[pl.BlockSpec((tm, NA_WIDTH), row), pl.BlockSpec((tm, MLA_WIDTH), row),
                  pl.BlockSpec((tm, D_MODEL), row),
                  pl.BlockSpec((1, 6, D_MODEL), lambda i: (mod_map1(i, tm), 0, 0)),
                  pl.BlockSpec((NA_WIDTH, D_MODEL), const, pipeline_mode=whole),
                  pl.BlockSpec((MLA_WIDTH, D_MODEL), lambda i: (1, 0), pipeline_mode=whole),
                  pl.BlockSpec((1, D_MODEL), const), pl.BlockSpec((1, D_MODEL), const)] + c_in,
        out_specs=[pl.BlockSpec((tm, D_MODEL), row)] + c_out,
        compiler_params=_cparams(("parallel",)),
        name="out_projection_ln1",
    )(ana, amla, x2d, mod3, w_o, w_o, g, b, *cast_weights)


def _ffn_kernel(x_ref, mod_ref, wg_ref, wu_ref, wd_ref, g_ref, b_ref, o_ref, h_scr):
    j = pl.program_id(1)

    @pl.when(j == 0)
    def _():
        h_scr[...] = (x_ref[...] * (1.0 + mod_ref[0, 4:5, :]) + mod_ref[0, 3:4, :]).astype(BF)
        o_ref[...] = jnp.zeros_like(o_ref)

    hb = h_scr[...]
    gate = _dot(hb, wg_ref[...])
    up = _dot(hb, wu_ref[...])
    act = (gate * _sigmoid(gate) * up).astype(BF)
    for n in range(D_MODEL // FFN_TN):
        sl = slice(n * FFN_TN, (n + 1) * FFN_TN)
        o_ref[:, sl] += _dot(act, wd_ref[:, sl])

    @pl.when(j == pl.num_programs(1) - 1)
    def _():
        y = ALPHA * x_ref[...] + mod_ref[0, 5:6, :] * o_ref[...]
        o_ref[...] = _layernorm(y, g_ref[...], b_ref[...])


def _ffn(x1, mod3, mod_map1, w_gu, w_down, g, b):
    tm, tf = FFN_TM, FFN_TF
    nf = D_FF // tf
    const = lambda i, j: (0, 0)
    row = lambda i, j: (i, 0)
    return pl.pallas_call(
        _ffn_kernel,
        out_shape=jax.ShapeDtypeStruct((N_TOK, D_MODEL), F32),
        grid=(N_TOK // tm, nf),
        in_specs=[pl.BlockSpec((tm, D_MODEL), row),
                  pl.BlockSpec((1, 6, D_MODEL), lambda i, j: (mod_map1(i, tm), 0, 0)),
                  pl.BlockSpec((D_MODEL, tf), lambda i, j: (0, j)),
                  pl.BlockSpec((D_MODEL, tf), lambda i, j: (0, j + nf)),
                  pl.BlockSpec((tf, D_MODEL), lambda i, j: (j, 0)),
                  pl.BlockSpec((1, D_MODEL), const), pl.BlockSpec((1, D_MODEL), const)],
        out_specs=pl.BlockSpec((tm, D_MODEL), row),
        scratch_shapes=[pltpu.VMEM((tm, D_MODEL), BF)],
        compiler_params=_cparams(("parallel", "arbitrary"), FFN_VMEM_LIMIT),
        name="swiglu_ln2",
    )(x1, mod3, w_gu, w_gu, w_down, g, b)


def _prep_weights(w_main_b, w_rope_b, q_a_norm, kv_a_norm, w_q_b, w_kv_b):
    wq = w_q_b.astype(BF).reshape(Q_LORA, MLA_HEADS, MLA_NOPE_DIM + MLA_ROPE_DIM)
    wkv = w_kv_b.astype(BF).reshape(KV_LORA, MLA_HEADS, MLA_NOPE_DIM + MLA_V_DIM)
    pad = LANES - MLA_ROPE_DIM
    return {
        "w_main": w_main_b,
        "w_rope": w_rope_b,
        "q_g": q_a_norm.reshape(1, Q_LORA),
        "kv_g": kv_a_norm.reshape(1, KV_LORA),
        "w_qn": wq[:, :, :MLA_NOPE_DIM].reshape(Q_LORA, NA_WIDTH),
        "w_qr": jnp.pad(wq[:, :, MLA_NOPE_DIM:], ((0, 0), (0, 0), (0, pad))).reshape(Q_LORA, NA_WIDTH),
        "w_kn": wkv[:, :, :MLA_NOPE_DIM].reshape(KV_LORA, NA_WIDTH),
        "w_v": wkv[:, :, MLA_NOPE_DIM:].reshape(KV_LORA, MLA_WIDTH),
    }


def _rope_tables():
    half = MLA_ROPE_DIM // 2
    inv_freq = np.float32(ROPE_THETA) ** (-np.arange(0, half, 2, dtype=np.float32) / np.float32(half))
    t = np.arange(DEC_SEQ, dtype=np.int32)

    def tables(pos):
        ang = pos.astype(np.float32)[:, None] * inv_freq
        s = np.sin(ang)
        return np.concatenate([np.cos(ang)] * 2, -1), np.concatenate([-s, s], -1)

    cr, sr = tables(t // GRID_W)
    cc, sc = tables(t % GRID_W)
    pad = LANES - MLA_ROPE_DIM
    cos = np.concatenate([cr, cc, np.ones((DEC_SEQ, pad), np.float32)], -1)
    sin = np.concatenate([sr, sc, np.zeros((DEC_SEQ, pad), np.float32)], -1)
    return jnp.asarray(cos, F32), jnp.asarray(sin, F32)


def _na_bias_plan():
    plan = []
    for step in NA_VARIANT_STEPS:
        start = min(max(NA_QROWS * step - NA_WIN_ROWS // 2, 0), GRID_H - NA_KROWS)
        per_row = []
        for i in range(NA_QROWS):
            r = NA_QROWS * step + i
            rs = min(max(r - NA_WIN_ROWS // 2, 0), GRID_H - NA_WIN_ROWS)
            per_row.append([(start + u - r + NA_WIN_ROWS - 1, rs <= start + u < rs + NA_WIN_ROWS)
                            for u in range(NA_KROWS)])
        plan.append(per_row)
    return plan


def _na_bias_kernel(ra_ref, rb_ref, o_ref):
    shape = (GRID_W, LANES)
    lane = lax.broadcasted_iota(jnp.int32, shape, 1)
    qc = lax.broadcasted_iota(jnp.int32, shape, 0)
    kc = lane % GRID_W
    cs = jnp.clip(qc - NA_WIN_COLS // 2, 0, GRID_W - NA_WIN_COLS)
    ok_col = (kc >= cs) & (kc < cs + NA_WIN_COLS)
    left = lane < GRID_W
    neg = jnp.full(shape, NEG, F32)
    for v, per_row in enumerate(_na_bias_plan()):
        for i, rows in enumerate(per_row):
            for t in range(len(rows) // 2):
                (dra, oka), (drb, okb) = rows[2 * t], rows[2 * t + 1]
                if oka and okb:
                    ok = ok_col
                elif oka:
                    ok = ok_col & left
                elif okb:
                    ok = ok_col & jnp.logical_not(left)
                for h in range(NA_HEADS):
                    if oka or okb:
                        src = jnp.zeros((1, LANES), F32)
                        if oka:
                            src = src + ra_ref[h, dra:dra + 1, :]
                        if okb:
                            src = src + rb_ref[h, drb:drb + 1, :]
                        val = pltpu.roll(jnp.broadcast_to(src, shape), LANES - (NA_WIN_COLS - 1), 1,
                                         stride=1, stride_axis=0)
                        tile = jnp.where(ok, val * LOG2E, neg)
                    else:
                        tile = neg
                    o_ref[v, h, i * GRID_W:(i + 1) * GRID_W, t * LANES:(t + 1) * LANES] = tile


def _na_bias_tables(rpb):
    n_dr, n_dc = 2 * NA_WIN_ROWS - 1, 2 * NA_WIN_COLS - 1
    ra = jnp.pad(rpb, ((0, 0), (0, 16 - n_dr), (0, LANES - n_dc)))
    rb = jnp.pad(rpb, ((0, 0), (0, 16 - n_dr), (GRID_W, LANES - GRID_W - n_dc)))
    return pl.pallas_call(
        _na_bias_kernel,
        out_shape=jax.ShapeDtypeStruct((len(NA_VARIANT_STEPS), NA_HEADS, NA_QTOK, NA_WIN_TOK), F32),
        compiler_params=pltpu.CompilerParams(vmem_limit_bytes=VMEM_LIMIT),
        name="na_bias_tables",
    )(ra, rb)


def kernel(x_prompt, x_sample, cache_na_k, cache_na_v, cache_mla_ckv, cache_mla_krope, c, c_ctx,
           w_mod, b_mod, w_in, q_a_norm, kv_a_norm, w_q_b, w_kv_b, na_rpb, w_o,
           ln1_g, ln1_b, w_gu, w_down, ln2_g, ln2_b):
    cond8 = jnp.concatenate([c_ctx[None], c, jnp.zeros((8 - 1 - DEC_BATCH, D_MODEL), F32)], 0)
    mod, w_main_b, w_rope_b = _modulation(cond8, w_mod[0], b_mod[0][None], w_in[0].T)
    mod3 = mod.reshape(8, 6, D_MODEL)

    w = _prep_weights(w_main_b, w_rope_b, q_a_norm[0], kv_a_norm[0], w_q_b[0], w_kv_b[0])
    g1, b1 = ln1_g[0][None], ln1_b[0][None]
    g2, b2 = ln2_g[0][None], ln2_b[0][None]
    cos, sin = _rope_tables()

    xp = x_prompt.reshape(N_TOK, D_MODEL)
    xs = x_sample.reshape(N_TOK, D_MODEL)
    ctx_mod = lambda i, tm: 0
    lat_mod = lambda i, tm: 1 + i // (DEC_SEQ // tm)

    naq, nak, nav, lat, kr, st_k, st_v = _projection(xp, mod3, ctx_mod, w, fused=False)
    st_kr = kr[:, :MLA_ROPE_DIM]
    a_na, a_mla, st_ckv, w_o_b = _ctx_attention(naq, nak, nav, lat, kr, w, (w_o[0],))
    x1_ctx, w_down_b = _out_projection(a_na, a_mla, xp, mod3, ctx_mod, w_o_b, g1, b1, (w_down[0],))

    naq, nak, nav, qm, km, vm = _projection(xs, mod3, lat_mod, w, fused=True, cos=cos, sin=sin)
    kr_cache = jnp.pad(cache_mla_krope[:, 0].reshape(DEC_BATCH * PAST_LEN, MLA_ROPE_DIM),
                       ((0, 0), (0, LANES - MLA_ROPE_DIM)))
    kmc, vmc = _cache_kv(cache_mla_ckv[:, 0].reshape(DEC_BATCH * PAST_LEN, KV_LORA), kr_cache, w)
    kc = cache_na_k[:, 0].reshape(DEC_BATCH * PAST_LEN, NA_WIDTH).astype(BF)
    vc = cache_na_v[:, 0].reshape(DEC_BATCH * PAST_LEN, NA_WIDTH).astype(BF)
    a_na = _lat_na_attention(naq, nak, nav, kc, vc, _na_bias_tables(na_rpb[0]))
    a_mla, w_gu_b = _lat_mla_attention(qm, km, vm, kmc, vmc, (w_gu[0],))
    x1_lat, = _out_projection(a_na, a_mla, xs, mod3, lat_mod, w_o_b, g1, b1)
    yp = _ffn(x1_ctx, mod3, ctx_mod, w_gu_b, w_down_b, g2, b2)
    ys = _ffn(x1_lat, mod3, lat_mod, w_gu_b, w_down_b, g2, b2)

    return (yp.reshape(BATCH, SEQ, D_MODEL),
            ys.reshape(DEC_BATCH, DEC_SEQ, D_MODEL),
            st_k.reshape(BATCH, 1, SEQ, NA_HEADS, NA_HEAD_DIM),
            st_v.reshape(BATCH, 1, SEQ, NA_HEADS, NA_HEAD_DIM),
            st_ckv.reshape(BATCH, 1, SEQ, KV_LORA),
            st_kr.reshape(BATCH, 1, SEQ, MLA_ROPE_DIM))
```
